```python
import jax, jax.numpy as jnp
from jax import lax
import numpy as np

D_MODEL = 1024
BATCH = 4
SEQ = 4096
DEPTH = 1

N_Q_HEADS = 16
N_KV_HEADS = 2
HEAD_DIM = 64
Q_PER_KV = N_Q_HEADS // N_KV_HEADS
WINDOW = 128
BLOCK = 128
ROPE_THETA = 500000.0
ROPE_DIM = HEAD_DIM // 4
ATTN_W = N_Q_HEADS * HEAD_DIM
KV_W = N_KV_HEADS * HEAD_DIM
CONV_CH = D_MODEL
CONV_WIDTH = 31
D_FF = 2816
PLE_DIM = 256
EPS = 1e-6
NEG_INF = -1e30
SPLIT_SIZES = (ATTN_W, KV_W, KV_W, 2 * CONV_CH, 2 * D_MODEL)
D_IN = sum(SPLIT_SIZES)

kernel_name = "hybrid_swa_sink_conformer_conv_macaron_ple"


def rmsnorm(x, g):
    xf = x.astype(jnp.float32)
    y = xf * lax.rsqrt(jnp.mean(xf * xf, axis=-1, keepdims=True) + EPS) * g.astype(jnp.float32)
    return y.astype(x.dtype)


def layernorm(x, g, b):
    xf = x.astype(jnp.float32)
    mu = jnp.mean(xf, axis=-1, keepdims=True)
    xc = xf - mu
    var = jnp.mean(xc * xc, axis=-1, keepdims=True)
    y = xc * lax.rsqrt(var + EPS) * g.astype(jnp.float32) + b.astype(jnp.float32)
    return y.astype(x.dtype)


def swiglu(h, w_gu, w_down):
    g, u = jnp.split(h @ w_gu, 2, axis=-1)
    return (jax.nn.silu(g) * u) @ w_down


def partial_rope(t, cos, sin):
    half = ROPE_DIM // 2
    t1 = t[..., :half]
    t2 = t[..., half:ROPE_DIM]
    rot = jnp.concatenate([t1 * cos - t2 * sin, t2 * cos + t1 * sin], axis=-1)
    return jnp.concatenate([rot.astype(t.dtype), t[..., ROPE_DIM:]], axis=-1)


def rope_tables(seq):
    inv_freq = ROPE_THETA ** (-jnp.arange(0, ROPE_DIM, 2, dtype=jnp.float32) / ROPE_DIM)
    ang = jnp.arange(seq, dtype=jnp.float32)[:, None] * inv_freq[None, :]
    return jnp.cos(ang)[None, :, None, :], jnp.sin(ang)[None, :, None, :]


def sliding_window_attention(q, k, v, sinks):
    B, S = q.shape[0], q.shape[1]
    nb = S // BLOCK
    qb = q.reshape(B, nb, BLOCK, N_KV_HEADS, Q_PER_KV, HEAD_DIM)

    def band(t):
        tb = t.reshape(B, nb, BLOCK, N_KV_HEADS, HEAD_DIM)
        prev = jnp.concatenate([jnp.zeros_like(tb[:, :1]), tb[:, :-1]], axis=1)
        return jnp.concatenate([prev, tb], axis=2)

    kb, vb = band(k), band(v)
    scale = HEAD_DIM ** -0.5
    s = jnp.einsum('bnqhgd,bnkhd->bnhgqk', qb, kb,
                   preferred_element_type=jnp.float32) * scale
    qi = jnp.arange(BLOCK)[:, None]
    kj = jnp.arange(2 * BLOCK)[None, :]
    delta = qi + BLOCK - kj
    kpos = jnp.arange(nb)[:, None, None] * BLOCK + kj[None] - BLOCK
    valid = (delta >= 0) & (delta < WINDOW) & (kpos >= 0)
    s = jnp.where(valid[None, :, None, None], s, NEG_INF)
    sink = sinks.astype(jnp.float32).reshape(N_KV_HEADS, Q_PER_KV)[None, None, :, :, None, None]
    m = jnp.maximum(jnp.max(s, axis=-1, keepdims=True), sink)
    pr = jnp.exp(s - m)
    denom = jnp.sum(pr, axis=-1, keepdims=True) + jnp.exp(sink - m)
    o = jnp.einsum('bnhgqk,bnkhd->bnqhgd', pr / denom, vb.astype(jnp.float32))
    return o.reshape(B, S, ATTN_W).astype(q.dtype)


def causal_depthwise_conv(u, w_dw, b_dw):
    up = jnp.pad(u, ((0, 0), (CONV_WIDTH - 1, 0), (0, 0)))
    y = lax.conv_general_dilated(up, w_dw[:, None, :], window_strides=(1,), padding='VALID',
                                 dimension_numbers=('NWC', 'WIO', 'NWC'),
                                 feature_group_count=CONV_CH)
    return y + b_dw


def setup_inputs(seed: int = 0) -> dict:
    key = jax.random.key(seed)
    ks = jax.random.split(key, 32)

    def nrm(k, shape, scale):
        return jax.random.normal(k, shape, jnp.float32) * scale

    def gain(k, shape):
        return 1.0 + 0.05 * jax.random.normal(k, shape, jnp.float32)

    L = DEPTH
    return {
        "x": nrm(ks[0], (BATCH, SEQ, D_MODEL), 1.0),
        "p": nrm(ks[1], (DEPTH, BATCH, SEQ, PLE_DIM), 1.0),
        "ffn1_norm": gain(ks[2], (L, D_MODEL)),
        "ffn1_wgu": nrm(ks[3], (L, D_MODEL, 2 * D_FF), D_MODEL ** -0.5),
        "ffn1_wdown": nrm(ks[4], (L, D_FF, D_MODEL), D_FF ** -0.5),
        "mix_norm": gain(ks[5], (L, D_MODEL)),
        "w_in": nrm(ks[6], (L, D_MODEL, D_IN), D_MODEL ** -0.5),
        "q_norm": gain(ks[7], (L, HEAD_DIM)),
        "k_norm": gain(ks[8], (L, HEAD_DIM)),
        "sinks": nrm(ks[9], (L, N_Q_HEADS), 0.5),
        "attn_proj": nrm(ks[10], (L, ATTN_W, D_MODEL), ATTN_W ** -0.5),
        "dw_w": nrm(ks[11], (L, CONV_WIDTH, CONV_CH), CONV_WIDTH ** -0.5),
        "dw_b": nrm(ks[12], (L, CONV_CH), 0.02),
        "conv_ln_g": gain(ks[13], (L, CONV_CH)),
        "conv_ln_b": nrm(ks[14], (L, CONV_CH), 0.02),
        "conv_proj": nrm(ks[15], (L, CONV_CH, D_MODEL), CONV_CH ** -0.5),
        "w_out": nrm(ks[16], (L, D_MODEL, D_MODEL), D_MODEL ** -0.5),
        "ffn2_norm": gain(ks[17], (L, D_MODEL)),
        "ffn2_wgu": nrm(ks[18], (L, D_MODEL, 2 * D_FF), D_MODEL ** -0.5),
        "ffn2_wdown": nrm(ks[19], (L, D_FF, D_MODEL), D_FF ** -0.5),
        "ple_proj": nrm(ks[20], (L, PLE_DIM, D_MODEL), PLE_DIM ** -0.5),
        "ple_norm": gain(ks[21], (L, D_MODEL)),
        "ple_gate_norm": gain(ks[22], (L, D_MODEL)),
        "ple_gate_w": nrm(ks[23], (L, D_MODEL, D_MODEL), D_MODEL ** -0.5),
    }


def reference(x, p, ffn1_norm, ffn1_wgu, ffn1_wdown, mix_norm, w_in, q_norm, k_norm, sinks,
              attn_proj, dw_w, dw_b, conv_ln_g, conv_ln_b, conv_proj, w_out,
              ffn2_norm, ffn2_wgu, ffn2_wdown, ple_proj, ple_norm, ple_gate_norm, ple_gate_w):
    B, S = x.shape[0], x.shape[1]
    cos, sin = rope_tables(S)
    offs = np.cumsum((0,) + SPLIT_SIZES)
    for i in range(DEPTH):
        x = x + 0.5 * swiglu(rmsnorm(x, ffn1_norm[i]), ffn1_wgu[i], ffn1_wdown[i])

        h = rmsnorm(x, mix_norm[i])
        z = h @ w_in[i]
        zq, zk, zv, zc, zg = [z[..., offs[j]:offs[j + 1]] for j in range(len(SPLIT_SIZES))]

        q = rmsnorm(zq.reshape(B, S, N_Q_HEADS, HEAD_DIM), q_norm[i])
        k = rmsnorm(zk.reshape(B, S, N_KV_HEADS, HEAD_DIM), k_norm[i])
        v = zv.reshape(B, S, N_KV_HEADS, HEAD_DIM)
        q = partial_rope(q, cos, sin)
        k = partial_rope(k, cos, sin)
        a = sliding_window_attention(q, k, v, sinks[i]) @ attn_proj[i]

        ca, cb = jnp.split(zc, 2, axis=-1)
        u = ca * jax.nn.sigmoid(cb)
        u = causal_depthwise_conv(u, dw_w[i], dw_b[i])
        c = jax.nn.silu(layernorm(u, conv_ln_g[i], conv_ln_b[i])) @ conv_proj[i]

        ga, gb = jnp.split(zg, 2, axis=-1)
        merged = jax.nn.sigmoid(ga) * a + jax.nn.sigmoid(gb) * c
        x = x + merged @ w_out[i]

        x = x + 0.5 * swiglu(rmsnorm(x, ffn2_norm[i]), ffn2_wgu[i], ffn2_wdown[i])

        e = rmsnorm(p[i] @ ple_proj[i], ple_norm[i])
        gate = jax.nn.sigmoid(rmsnorm(x, ple_gate_norm[i]) @ ple_gate_w[i])
        x = x + gate * e
    return x
```

```python
import functools

import jax
import jax.numpy as jnp
import numpy as np
from jax import lax
from jax.experimental import pallas as pl
from jax.experimental.pallas import tpu as pltpu

D_MODEL = 1024
N_Q_HEADS = 16
N_KV_HEADS = 2
HEAD_DIM = 64
WINDOW = 128
BLOCK = 128
ROPE_THETA = 500000.0
ROPE_DIM = HEAD_DIM // 4
ATTN_W = N_Q_HEADS * HEAD_DIM
KV_W = N_KV_HEADS * HEAD_DIM
CONV_WIDTH = 31
D_FF = 2816
PLE_DIM = 256
EPS = 1e-6
NEG_INF = -1e30

LANES = 128
CONV_HALO = 32
VMEM_LIMIT = 56 * 1024 * 1024

BF16 = jnp.bfloat16
F32 = jnp.float32


def _dot(a, b):
    return jnp.dot(a, b, preferred_element_type=F32)


def _rmsnorm(x, g):
    return x * lax.rsqrt(jnp.mean(x * x, axis=-1, keepdims=True) + EPS) * g


def _const_spec(shape):
    return pl.BlockSpec(shape, lambda *_: (0,) * len(shape), pipeline_mode=pl.Buffered(1))


def _params(n_axes):
    return pltpu.CompilerParams(dimension_semantics=("arbitrary",) * n_axes,
                                vmem_limit_bytes=VMEM_LIMIT)


def _swiglu_update(x, g_ref, wg_ref, wu_ref, wd_ref):
    h = _rmsnorm(x, g_ref[...]).astype(BF16)
    g = _dot(h, wg_ref[...])
    u = _dot(h, wu_ref[...])
    a = (g * jax.nn.sigmoid(g) * u).astype(BF16)
    return x + 0.5 * _dot(a, wd_ref[...])


def _ffn_kernel(x_ref, g_ref, wg_ref, wu_ref, wd_ref, o_ref):
    o_ref[...] = _swiglu_update(x_ref[...], g_ref, wg_ref, wu_ref, wd_ref)


def _ffn(x, norm, wg, wu, wd, tm):
    t = x.shape[0]
    row = pl.BlockSpec((tm, D_MODEL), lambda i: (i, 0))
    return pl.pallas_call(
        _ffn_kernel,
        grid=(t // tm,),
        in_specs=[row, _const_spec((1, D_MODEL)), _const_spec((D_MODEL, D_FF)),
                  _const_spec((D_MODEL, D_FF)), _const_spec((D_FF, D_MODEL))],
        out_specs=row,
        out_shape=jax.ShapeDtypeStruct((t, D_MODEL), F32),
        compiler_params=_params(1),
        name="ffn1",
    )(x, norm, wg, wu, wd)


def _ffn_ple_kernel(x_ref, p_ref, g_ref, wg_ref, wu_ref, wd_ref, pw_ref, pn_ref, gn_ref, gw_ref, o_ref):
    x = _swiglu_update(x_ref[...], g_ref, wg_ref, wu_ref, wd_ref)
    e = _rmsnorm(_dot(p_ref[...].astype(BF16), pw_ref[...]), pn_ref[...])
    gate = jax.nn.sigmoid(_dot(_rmsnorm(x, gn_ref[...]).astype(BF16), gw_ref[...]))
    o_ref[...] = x + gate * e


def _ffn_ple(x, p, norm, wg, wu, wd, ple_w, ple_norm, gate_norm, gate_w, tm):
    t = x.shape[0]
    row = pl.BlockSpec((tm, D_MODEL), lambda i: (i, 0))
    vec = _const_spec((1, D_MODEL))
    return pl.pallas_call(
        _ffn_ple_kernel,
        grid=(t // tm,),
        in_specs=[row, pl.BlockSpec((tm, PLE_DIM), lambda i: (i, 0)), vec,
                  _const_spec((D_MODEL, D_FF)), _const_spec((D_MODEL, D_FF)), _const_spec((D_FF, D_MODEL)),
                  _const_spec((PLE_DIM, D_MODEL)), vec, vec, _const_spec((D_MODEL, D_MODEL))],
        out_specs=row,
        out_shape=jax.ShapeDtypeStruct((t, D_MODEL), F32),
        compiler_params=_params(1),
        name="ffn2_ple",
    )(x, p, norm, wg, wu, wd, ple_w, ple_norm, gate_norm, gate_w)


def _head_norm_rope(z, ones_blk, gain, c_tab, s_fwd, s_bwd):
    w = z.shape[1]
    blk = ones_blk.shape[0]
    sq = (z * z).astype(BF16)
    ss = jnp.concatenate([_dot(sq[:, i:i + blk], ones_blk) for i in range(0, w, blk)], axis=1)
    t = z * lax.rsqrt(ss * (1.0 / HEAD_DIM) + EPS) * gain
    half = ROPE_DIM // 2
    up = pltpu.roll(t, w - half, axis=1)
    dn = pltpu.roll(t, half, axis=1)
    reps = w // LANES
    tile = lambda a: jnp.concatenate([a] * reps, axis=1) if reps > 1 else a
    return t * tile(c_tab) + up * tile(s_fwd) + dn * tile(s_bwd)


def _proj_in_kernel(x_ref, g_ref, wq_ref, wkv_ref, wc_ref, wg_ref, oq_ref, ok_ref, gq_ref, gk_ref,
                    c_ref, sf_ref, sb_ref, q_out, k_out, v_out, u_out, ga_out, gb_out):
    h = _rmsnorm(x_ref[...], g_ref[...]).astype(BF16)
    c_tab, s_fwd, s_bwd = c_ref[...], sf_ref[...], sb_ref[...]
    zq = _dot(h, wq_ref[...])
    scale = HEAD_DIM ** -0.5
    q_out[...] = (_head_norm_rope(zq, oq_ref[...], gq_ref[...], c_tab, s_fwd, s_bwd) * scale).astype(BF16)
    zkv = _dot(h, wkv_ref[...])
    k_out[...] = _head_norm_rope(zkv[:, :KV_W], ok_ref[...], gk_ref[...], c_tab, s_fwd, s_bwd).astype(BF16)
    v_out[...] = zkv[:, KV_W:].astype(BF16)
    zc = _dot(h, wc_ref[...])
    u_out[...] = (zc[:, :D_MODEL] * jax.nn.sigmoid(zc[:, D_MODEL:])).astype(BF16)
    zg = _dot(h, wg_ref[...])
    ga_out[...] = jax.nn.sigmoid(zg[:, :D_MODEL]).astype(BF16)
    gb_out[...] = jax.nn.sigmoid(zg[:, D_MODEL:]).astype(BF16)


def _proj_in(x, norm, wq, wkv, wc, wg, ones_q, ones_k, gq, gk, c_tab, s_fwd, s_bwd, tm, seq):
    t = x.shape[0]
    tiles_per_seq = seq // tm
    row = lambda w: pl.BlockSpec((tm, w), lambda i: (i, 0))
    tab = pl.BlockSpec((tm, LANES), lambda i: (i % tiles_per_seq, 0))
    out = lambda w: jax.ShapeDtypeStruct((t, w), BF16)
    return pl.pallas_call(
        _proj_in_kernel,
        grid=(t // tm,),
        in_specs=[row(D_MODEL), _const_spec((1, D_MODEL)),
                  _const_spec(wq.shape), _const_spec(wkv.shape), _const_spec(wc.shape), _const_spec(wg.shape),
                  _const_spec(ones_q.shape), _const_spec(ones_k.shape),
                  _const_spec((1, ATTN_W)), _const_spec((1, KV_W)), tab, tab, tab],
        out_specs=[row(ATTN_W), row(KV_W), row(KV_W), row(D_MODEL), row(D_MODEL), row(D_MODEL)],
        out_shape=[out(ATTN_W), out(KV_W), out(KV_W), out(D_MODEL), out(D_MODEL), out(D_MODEL)],
        compiler_params=_params(1),
        name="proj_in",
    )(x, norm, wq, wkv, wc, wg, ones_q, ones_k, gq, gk, c_tab, s_fwd, s_bwd)


def _attn_kernel(sinks_ref, q_ref, k_ref, v_ref, o_ref, *, q_tile):
    tile_start = pl.program_id(1) * q_tile
    lane = lax.broadcasted_iota(jnp.int32, (2 * BLOCK, LANES), 1)
    lo = lane < HEAD_DIM
    tiles_per_kv = ATTN_W // LANES // N_KV_HEADS
    qi = lax.broadcasted_iota(jnp.int32, (tiles_per_kv * BLOCK, 2 * BLOCK), 0) % BLOCK
    kj = lax.broadcasted_iota(jnp.int32, (tiles_per_kv * BLOCK, 2 * BLOCK), 1)
    delta = qi + BLOCK - kj
    band = (delta >= 0) & (delta < WINDOW)
    out_lo = lax.broadcasted_iota(jnp.int32, (tiles_per_kv * BLOCK, LANES), 1) < HEAD_DIM

    for blk in range(q_tile // BLOCK):
        start = pl.multiple_of(tile_start + blk * BLOCK, BLOCK)
        prev = pl.multiple_of(jnp.maximum(start - BLOCK, 0), BLOCK)
        valid = band & (start + kj - BLOCK >= 0)
        rows = pl.ds(blk * BLOCK, BLOCK)

        def band_of(ref):
            return jnp.concatenate([ref[pl.ds(prev, BLOCK), :], ref[pl.ds(start, BLOCK), :]], axis=0)

        kf, vf = band_of(k_ref), band_of(v_ref)
        kr, vr = pltpu.roll(kf, HEAD_DIM, axis=1), pltpu.roll(vf, HEAD_DIM, axis=1)
        zero = jnp.zeros_like(kf)
        for h in range(N_KV_HEADS):
            k_a, k_b = (kf, kr) if h == 0 else (kr, kf)
            v_a, v_b = (vf, vr) if h == 0 else (vr, vf)
            k_hat = jnp.concatenate([jnp.where(lo, k_a, zero), jnp.where(lo, zero, k_b)], axis=0)
            v_hat = jnp.concatenate([jnp.where(lo, v_a, zero), jnp.where(lo, zero, v_b)], axis=0)
            q_h = jnp.concatenate(
                [q_ref[rows, pl.ds((h * tiles_per_kv + c) * LANES, LANES)] for c in range(tiles_per_kv)], axis=0)
            s = lax.dot_general(q_h, k_hat, (((1,), (1,)), ((), ())), preferred_element_type=F32)
            probs, inv = [], []
            for half in range(2):
                sh = jnp.where(valid, s[:, half * 2 * BLOCK:(half + 1) * 2 * BLOCK], NEG_INF)
                sink = jnp.concatenate(
                    [jnp.full((BLOCK, 1), sinks_ref[2 * (h * tiles_per_kv + c) + half], F32)
                     for c in range(tiles_per_kv)], axis=0)
                m = jnp.maximum(jnp.max(sh, axis=-1, keepdims=True), sink)
                p = jnp.exp(sh - m)
                denom = jnp.sum(p, axis=-1, keepdims=True) + jnp.exp(sink - m)
                probs.append(p.astype(BF16))
                inv.append(1.0 / denom)
            o2 = _dot(jnp.concatenate(probs, axis=1), v_hat)
            o2 = (o2 * jnp.where(out_lo, inv[0], inv[1])).astype(BF16)
            for c in range(tiles_per_kv):
                o_ref[rows, pl.ds((h * tiles_per_kv + c) * LANES, LANES)] = o2[c * BLOCK:(c + 1) * BLOCK]


def _attention(q, k, v, sinks, batch, seq, q_tile):
    t = q.shape[0]
    tiles = seq // q_tile
    return pl.pallas_call(
        functools.partial(_attn_kernel, q_tile=q_tile),
        grid_spec=pltpu.PrefetchScalarGridSpec(
            num_scalar_prefetch=1,
            grid=(batch, tiles),
            in_specs=[pl.BlockSpec((q_tile, ATTN_W), lambda b, i, s: (b * tiles + i, 0)),
                      pl.BlockSpec((seq, KV_W), lambda b, i, s: (b, 0)),
                      pl.BlockSpec((seq, KV_W), lambda b, i, s: (b, 0))],
            out_specs=pl.BlockSpec((q_tile, ATTN_W), lambda b, i, s: (b * tiles + i, 0)),
        ),
        out_shape=jax.ShapeDtypeStruct((t, ATTN_W), BF16),
        compiler_params=_params(2),
        name="swa",
    )(sinks, q, k, v)


def _merge_kernel(x_ref, a_ref, u_ref, halo_ref, ga_ref, gb_ref, dw_ref, db_ref, lg_ref, lb_ref,
                  wa_ref, wc_ref, wo_ref, o_ref, uf_ref, *, tm):
    first = pl.program_id(1) == 0
    halo = halo_ref[...].astype(F32)
    uf_ref[pl.ds(0, CONV_HALO), :] = jnp.where(first, jnp.zeros_like(halo), halo)
    uf_ref[pl.ds(CONV_HALO, tm), :] = u_ref[...].astype(F32)
    base = CONV_HALO - (CONV_WIDTH - 1)
    y = uf_ref[pl.ds(base, tm), :] * dw_ref[0:1, :]
    for j in range(1, CONV_WIDTH):
        y = y + uf_ref[pl.ds(base + j, tm), :] * dw_ref[j:j + 1, :]
    y = y + db_ref[...]
    mu = jnp.mean(y, axis=-1, keepdims=True)
    yc = y - mu
    var = jnp.mean(yc * yc, axis=-1, keepdims=True)
    ln = yc * lax.rsqrt(var + EPS) * lg_ref[...] + lb_ref[...]
    c_act = (ln * jax.nn.sigmoid(ln)).astype(BF16)
    c = _dot(c_act, wc_ref[...])
    a = _dot(a_ref[...], wa_ref[...])
    merged = (ga_ref[...].astype(F32) * a + gb_ref[...].astype(F32) * c).astype(BF16)
    o_ref[...] = x_ref[...] + _dot(merged, wo_ref[...])


def _merge(x, a, u, ga, gb, dw_w, dw_b, ln_g, ln_b, wa, wc, wo, batch, seq, tm):
    t = x.shape[0]
    tiles = seq // tm
    halo_per_tile = tm // CONV_HALO
    row = pl.BlockSpec((tm, D_MODEL), lambda b, i: (b * tiles + i, 0))
    halo = pl.BlockSpec((CONV_HALO, D_MODEL),
                        lambda b, i: (jnp.maximum((b * tiles + i) * halo_per_tile - 1, 0), 0))
    vec = _const_spec((1, D_MODEL))
    mat = _const_spec((D_MODEL, D_MODEL))
    return pl.pallas_call(
        functools.partial(_merge_kernel, tm=tm),
        grid=(batch, tiles),
        in_specs=[row, row, row, halo, row, row, _const_spec((CONV_WIDTH, D_MODEL)), vec, vec, vec,
                  mat, mat, mat],
        out_specs=row,
        out_shape=jax.ShapeDtypeStruct((t, D_MODEL), F32),
        scratch_shapes=[pltpu.VMEM((CONV_HALO + tm, D_MODEL), F32)],
        compiler_params=_params(2),
        name="conv_merge",
    )(x, a, u, u, ga, gb, dw_w, dw_b, ln_g, ln_b, wa, wc, wo)


def _rope_tables(seq):
    half = ROPE_DIM // 2
    inv_freq = ROPE_THETA ** (-np.arange(0, ROPE_DIM, 2, dtype=np.float32) / ROPE_DIM)
    ang = jnp.arange(seq, dtype=F32)[:, None] * jnp.asarray(inv_freq, F32)[None, :]
    cos, sin = jnp.cos(ang), jnp.sin(ang)
    ones = jnp.ones((seq, HEAD_DIM - ROPE_DIM), F32)
    zeros = jnp.zeros((seq, HEAD_DIM - half), F32)
    c_head = jnp.concatenate([cos, cos, ones], axis=1)
    f_head = jnp.concatenate([-sin, zeros], axis=1)
    b_head = jnp.concatenate([jnp.zeros((seq, half), F32), sin, jnp.zeros((seq, HEAD_DIM - ROPE_DIM), F32)], axis=1)
    rep = lambda a: jnp.concatenate([a] * (LANES // HEAD_DIM), axis=1)
    return rep(c_head), rep(f_head), rep(b_head)


def _block_ones(width):
    head = np.arange(width) // HEAD_DIM
    return jnp.asarray(head[:, None] == head[None, :], BF16)


def kernel(x, p, ffn1_norm, ffn1_wgu, ffn1_wdown, mix_norm, w_in, q_norm, k_norm, sinks, attn_proj, dw_w, dw_b, conv_ln_g, conv_ln_b, conv_proj, w_out, ffn2_norm, ffn2_wgu, ffn2_wdown, ple_proj, ple_norm, ple_gate_norm, ple_gate_w):
    batch, seq, _ = x.shape
    depth = ffn1_norm.shape[0]
    t = batch * seq
    tm = 512
    xt = x.reshape(t, D_MODEL)
    c_tab, s_fwd, s_bwd = _rope_tables(seq)
    ones_q, ones_k = _block_ones(2 * LANES), _block_ones(KV_W)
    vec = lambda a: a.reshape(1, -1)
    bf = lambda a: a.astype(BF16)
    o_k = ATTN_W
    o_c = o_k + 2 * KV_W
    o_g = o_c + 2 * D_MODEL
    for i in range(depth):
        xt = _ffn(xt, vec(ffn1_norm[i]), bf(ffn1_wgu[i][:, :D_FF]), bf(ffn1_wgu[i][:, D_FF:]),
                  bf(ffn1_wdown[i]), tm)
        wi = w_in[i]
        q, k, v, u, ga, gb = _proj_in(
            xt, vec(mix_norm[i]), bf(wi[:, :o_k]), bf(wi[:, o_k:o_c]), bf(wi[:, o_c:o_g]), bf(wi[:, o_g:]),
            ones_q, ones_k, vec(jnp.tile(q_norm[i], N_Q_HEADS)), vec(jnp.tile(k_norm[i], N_KV_HEADS)),
            c_tab, s_fwd, s_bwd, tm, seq)
        a = _attention(q, k, v, sinks[i], batch, seq, tm)
        xt = _merge(xt, a, u, ga, gb, dw_w[i], vec(dw_b[i]), vec(conv_ln_g[i]), vec(conv_ln_b[i]),
                    bf(attn_proj[i]), bf(conv_proj[i]), bf(w_out[i]), batch, seq, tm)
        xt = _ffn_ple(xt, p[i].reshape(t, PLE_DIM), vec(ffn2_norm[i]), bf(ffn2_wgu[i][:, :D_FF]),
                      bf(ffn2_wgu[i][:, D_FF:]), bf(ffn2_wdown[i]), bf(ple_proj[i]), vec(ple_norm[i]),
                      vec(ple_gate_norm[i]), bf(ple_gate_w[i]), tm)
    return xt.reshape(batch, seq, D_MODEL)
```

```python
import functools

import jax
import jax.numpy as jnp
import numpy as np
from jax import lax
from jax.experimental import pallas as pl
from jax.experimental.pallas import tpu as pltpu

D_MODEL = 1024
N_Q_HEADS = 16
N_KV_HEADS = 2
HEAD_DIM = 64
WINDOW = 128
BLOCK = 128
ROPE_THETA = 500000.0
ROPE_DIM = HEAD_DIM // 4
ATTN_W = N_Q_HEADS * HEAD_DIM
KV_W = N_KV_HEADS * HEAD_DIM
CONV_WIDTH = 31
D_FF = 2816
PLE_DIM = 256
EPS = 1e-6
NEG_INF = -1e30

LANES = 128
SUBLANES = 8
assert D_MODEL == SUBLANES * LANES
CONV_HALO = 32
CONV_STEPS = 16
VMEM_LIMIT = 56 * 1024 * 1024

BF16 = jnp.bfloat16
F32 = jnp.float32


def _dot(a, b):
    return jnp.dot(a, b, preferred_element_type=F32)


def _rmsnorm(x, g):
    return x * lax.rsqrt(jnp.mean(x * x, axis=-1, keepdims=True) + EPS) * g


def _const_spec(shape):
    return pl.BlockSpec(shape, lambda *_: (0,) * len(shape), pipeline_mode=pl.Buffered(1))


def _params(n_axes):
    return pltpu.CompilerParams(dimension_semantics=("arbitrary",) * n_axes,
                                vmem_limit_bytes=VMEM_LIMIT)


def _swiglu_update(x, g_ref, wg_ref, wu_ref, wd_ref):
    h = _rmsnorm(x, g_ref[...]).astype(BF16)
    g = _dot(h, wg_ref[...])
    u = _dot(h, wu_ref[...])
    a = (g * jax.nn.sigmoid(g) * u).astype(BF16)
    return x + 0.5 * _dot(a, wd_ref[...])


def _ffn_kernel(x_ref, g_ref, wg_ref, wu_ref, wd_ref, o_ref):
    o_ref[...] = _swiglu_update(x_ref[...], g_ref, wg_ref, wu_ref, wd_ref)


def _ffn(x, norm, wg, wu, wd, tm):
    t = x.shape[0]
    row = pl.BlockSpec((tm, D_MODEL), lambda i: (i, 0))
    return pl.pallas_call(
        _ffn_kernel,
        grid=(t // tm,),
        in_specs=[row, _const_spec((1, D_MODEL)), _const_spec((D_MODEL, D_FF)),
                  _const_spec((D_MODEL, D_FF)), _const_spec((D_FF, D_MODEL))],
        out_specs=row,
        out_shape=jax.ShapeDtypeStruct((t, D_MODEL), F32),
        compiler_params=_params(1),
        name="ffn1",
    )(x, norm, wg, wu, wd)


def _ffn_ple_kernel(x_ref, p_ref, g_ref, wg_ref, wu_ref, wd_ref, pw_ref, pn_ref, gn_ref, gw_ref, o_ref):
    x = _swiglu_update(x_ref[...], g_ref, wg_ref, wu_ref, wd_ref)
    e = _rmsnorm(_dot(p_ref[...].astype(BF16), pw_ref[...]), pn_ref[...])
    gate = jax.nn.sigmoid(_dot(_rmsnorm(x, gn_ref[...]).astype(BF16), gw_ref[...]))
    o_ref[...] = x + gate * e


def _ffn_ple(x, p, norm, wg, wu, wd, ple_w, ple_norm, gate_norm, gate_w, tm):
    t = x.shape[0]
    row = pl.BlockSpec((tm, D_MODEL), lambda i: (i, 0))
    vec = _const_spec((1, D_MODEL))
    return pl.pallas_call(
        _ffn_ple_kernel,
        grid=(t // tm,),
        in_specs=[row, pl.BlockSpec((tm, PLE_DIM), lambda i: (i, 0)), vec,
                  _const_spec((D_MODEL, D_FF)), _const_spec((D_MODEL, D_FF)), _const_spec((D_FF, D_MODEL)),
                  _const_spec((PLE_DIM, D_MODEL)), vec, vec, _const_spec((D_MODEL, D_MODEL))],
        out_specs=row,
        out_shape=jax.ShapeDtypeStruct((t, D_MODEL), F32),
        compiler_params=_params(1),
        name="ffn2_ple",
    )(x, p, norm, wg, wu, wd, ple_w, ple_norm, gate_norm, gate_w)


def _head_norm_rope(z, ones_blk, gain, c_tab, s_fwd, s_bwd):
    w = z.shape[1]
    blk = ones_blk.shape[0]
    sq = (z * z).astype(BF16)
    ss = jnp.concatenate([_dot(sq[:, i:i + blk], ones_blk) for i in range(0, w, blk)], axis=1)
    t = z * lax.rsqrt(ss * (1.0 / HEAD_DIM) + EPS) * gain
    half = ROPE_DIM // 2
    up = pltpu.roll(t, w - half, axis=1)
    dn = pltpu.roll(t, half, axis=1)
    reps = w // LANES
    tile = lambda a: jnp.concatenate([a] * reps, axis=1) if reps > 1 else a
    return t * tile(c_tab) + up * tile(s_fwd) + dn * tile(s_bwd)


def _proj_in_kernel(x_ref, g_ref, wq_ref, wkv_ref, wc_ref, wg_ref, oq_ref, ok_ref, gq_ref, gk_ref,
                    c_ref, sf_ref, sb_ref, q_out, k_out, v_out, u_out, ga_out, gb_out):
    h = _rmsnorm(x_ref[...], g_ref[...]).astype(BF16)
    c_tab, s_fwd, s_bwd = c_ref[...], sf_ref[...], sb_ref[...]
    zq = _dot(h, wq_ref[...])
    scale = HEAD_DIM ** -0.5
    q_out[...] = (_head_norm_rope(zq, oq_ref[...], gq_ref[...], c_tab, s_fwd, s_bwd) * scale).astype(BF16)
    zkv = _dot(h, wkv_ref[...])
    k_out[...] = _head_norm_rope(zkv[:, :KV_W], ok_ref[...], gk_ref[...], c_tab, s_fwd, s_bwd).astype(BF16)
    v_out[...] = zkv[:, KV_W:].astype(BF16)
    zc = _dot(h, wc_ref[...])
    u_out[...] = (zc[:, :D_MODEL] * jax.nn.sigmoid(zc[:, D_MODEL:])).astype(BF16)
    zg = _dot(h, wg_ref[...])
    ga_out[...] = jax.nn.sigmoid(zg[:, :D_MODEL]).astype(BF16)
    gb_out[...] = jax.nn.sigmoid(zg[:, D_MODEL:]).astype(BF16)


def _proj_in(x, norm, wq, wkv, wc, wg, ones_q, ones_k, gq, gk, c_tab, s_fwd, s_bwd, tm, seq):
    t = x.shape[0]
    tiles_per_seq = seq // tm
    row = lambda w: pl.BlockSpec((tm, w), lambda i: (i, 0))
    tab = pl.BlockSpec((tm, LANES), lambda i: (i % tiles_per_seq, 0))
    out = lambda w: jax.ShapeDtypeStruct((t, w), BF16)
    return pl.pallas_call(
        _proj_in_kernel,
        grid=(t // tm,),
        in_specs=[row(D_MODEL), _const_spec((1, D_MODEL)),
                  _const_spec(wq.shape), _const_spec(wkv.shape), _const_spec(wc.shape), _const_spec(wg.shape),
                  _const_spec(ones_q.shape), _const_spec(ones_k.shape),
                  _const_spec((1, ATTN_W)), _const_spec((1, KV_W)), tab, tab, tab],
        out_specs=[row(ATTN_W), row(KV_W), row(KV_W), row(D_MODEL), row(D_MODEL), row(D_MODEL)],
        out_shape=[out(ATTN_W), out(KV_W), out(KV_W), out(D_MODEL), out(D_MODEL), out(D_MODEL)],
        compiler_params=_params(1),
        name="proj_in",
    )(x, norm, wq, wkv, wc, wg, ones_q, ones_k, gq, gk, c_tab, s_fwd, s_bwd)


def _attn_kernel(sinks_ref, q_ref, k_ref, v_ref, o_ref, *, q_tile):
    tile_start = pl.program_id(1) * q_tile
    lane = lax.broadcasted_iota(jnp.int32, (2 * BLOCK, LANES), 1)
    lo = lane < HEAD_DIM
    tiles_per_kv = ATTN_W // LANES // N_KV_HEADS
    qi = lax.broadcasted_iota(jnp.int32, (tiles_per_kv * BLOCK, 2 * BLOCK), 0) % BLOCK
    kj = lax.broadcasted_iota(jnp.int32, (tiles_per_kv * BLOCK, 2 * BLOCK), 1)
    delta = qi + BLOCK - kj
    band = (delta >= 0) & (delta < WINDOW)
    out_lo = lax.broadcasted_iota(jnp.int32, (tiles_per_kv * BLOCK, LANES), 1) < HEAD_DIM
    key_lo = lax.broadcasted_iota(jnp.int32, (2 * 2 * BLOCK, LANES), 0) < 2 * BLOCK
    ones_hat = (key_lo == out_lo).astype(BF16)

    for blk in range(q_tile // BLOCK):
        start = pl.multiple_of(tile_start + blk * BLOCK, BLOCK)
        prev = pl.multiple_of(jnp.maximum(start - BLOCK, 0), BLOCK)
        valid = band & (start + kj - BLOCK >= 0)
        rows = pl.ds(blk * BLOCK, BLOCK)

        def band_of(ref):
            return jnp.concatenate([ref[pl.ds(prev, BLOCK), :], ref[pl.ds(start, BLOCK), :]], axis=0)

        kf, vf = band_of(k_ref), band_of(v_ref)
        kr, vr = pltpu.roll(kf, HEAD_DIM, axis=1), pltpu.roll(vf, HEAD_DIM, axis=1)
        zero = jnp.zeros_like(kf)
        for h in range(N_KV_HEADS):
            k_a, k_b = (kf, kr) if h == 0 else (kr, kf)
            v_a, v_b = (vf, vr) if h == 0 else (vr, vf)
            k_hat = jnp.concatenate([jnp.where(lo, k_a, zero), jnp.where(lo, zero, k_b)], axis=0)
            v_hat = jnp.concatenate([jnp.where(lo, v_a, zero), jnp.where(lo, zero, v_b)], axis=0)
            q_h = jnp.concatenate(
                [q_ref[rows, pl.ds((h * tiles_per_kv + c) * LANES, LANES)] for c in range(tiles_per_kv)], axis=0)
            s = lax.dot_general(q_h, k_hat, (((1,), (1,)), ((), ())), preferred_element_type=F32)
            probs, sink_terms = [], []
            for half in range(2):
                sh = jnp.where(valid, s[:, half * 2 * BLOCK:(half + 1) * 2 * BLOCK], NEG_INF)
                sink = jnp.concatenate(
                    [jnp.full((BLOCK, LANES), sinks_ref[2 * (h * tiles_per_kv + c) + half], F32)
                     for c in range(tiles_per_kv)], axis=0)
                m = jnp.max(jnp.maximum(jnp.maximum(sh[:, :LANES], sh[:, LANES:]), sink), axis=-1, keepdims=True)
                probs.append(jnp.exp(sh - m).astype(BF16))
                sink_terms.append(jnp.exp(sink - m))
            o_aug = _dot(jnp.concatenate(probs, axis=1), jnp.concatenate([v_hat, ones_hat], axis=1))
            denom = o_aug[:, LANES:] + jnp.where(out_lo, sink_terms[0], sink_terms[1])
            o2 = (o_aug[:, :LANES] / denom).astype(BF16)
            for c in range(tiles_per_kv):
                o_ref[rows, pl.ds((h * tiles_per_kv + c) * LANES, LANES)] = o2[c * BLOCK:(c + 1) * BLOCK]


def _attention(q, k, v, sinks, batch, seq, q_tile):
    t = q.shape[0]
    tiles = seq // q_tile
    return pl.pallas_call(
        functools.partial(_attn_kernel, q_tile=q_tile),
        grid_spec=pltpu.PrefetchScalarGridSpec(
            num_scalar_prefetch=1,
            grid=(batch, tiles),
            in_specs=[pl.BlockSpec((q_tile, ATTN_W), lambda b, i, s: (b * tiles + i, 0)),
                      pl.BlockSpec((seq, KV_W), lambda b, i, s: (b, 0)),
                      pl.BlockSpec((seq, KV_W), lambda b, i, s: (b, 0))],
            out_specs=pl.BlockSpec((q_tile, ATTN_W), lambda b, i, s: (b * tiles + i, 0)),
        ),
        out_shape=jax.ShapeDtypeStruct((t, ATTN_W), BF16),
        compiler_params=_params(2),
        name="swa",
    )(sinks, q, k, v)


def _causal_conv(u_ref, tmaj_ref, y_ref, dw_ref, db_ref, first, tm):
    halo_rows = CONV_HALO * SUBLANES

    @pl.when(first)
    def _():
        tmaj_ref[pl.ds(0, halo_rows), :] = jnp.zeros((halo_rows, LANES), F32)

    @pl.when(jnp.logical_not(first))
    def _():
        tmaj_ref[pl.ds(0, halo_rows), :] = tmaj_ref[pl.ds(tm * SUBLANES, halo_rows), :]

    for c in range(SUBLANES):
        tmaj_ref[pl.ds(halo_rows + c, tm, stride=SUBLANES), :] = u_ref[:, c * LANES:(c + 1) * LANES].astype(F32)

    base = CONV_HALO - (CONV_WIDTH - 1)
    bias = db_ref[...]

    def block(i, carry):
        t0 = i * CONV_STEPS
        acc = [None] * CONV_STEPS
        for j in range(CONV_WIDTH):
            w_j = dw_ref[pl.ds(j * SUBLANES, SUBLANES), :]
            for r in range(CONV_STEPS):
                row = pl.multiple_of((t0 + r + base + j) * SUBLANES, SUBLANES)
                v = tmaj_ref[pl.ds(row, SUBLANES), :] * w_j
                acc[r] = v if j == 0 else acc[r] + v
        for r in range(CONV_STEPS):
            y_ref[pl.ds(pl.multiple_of((t0 + r) * SUBLANES, SUBLANES), SUBLANES), :] = acc[r] + bias
        return carry

    lax.fori_loop(0, tm // CONV_STEPS, block, 0)
    return jnp.concatenate([y_ref[pl.ds(c, tm, stride=SUBLANES), :] for c in range(SUBLANES)], axis=1)


def _merge_kernel(x_ref, a_ref, u_ref, ga_ref, gb_ref, dw_ref, db_ref, lg_ref, lb_ref,
                  wa_ref, wc_ref, wo_ref, o_ref, tmaj_ref, y_ref, *, tm):
    y = _causal_conv(u_ref, tmaj_ref, y_ref, dw_ref, db_ref, pl.program_id(1) == 0, tm)
    mu = jnp.mean(y, axis=-1, keepdims=True)
    yc = y - mu
    var = jnp.mean(yc * yc, axis=-1, keepdims=True)
    ln = yc * lax.rsqrt(var + EPS) * lg_ref[...] + lb_ref[...]
    c_act = (ln * jax.nn.sigmoid(ln)).astype(BF16)
    c = _dot(c_act, wc_ref[...])
    a = _dot(a_ref[...], wa_ref[...])
    merged = (ga_ref[...].astype(F32) * a + gb_ref[...].astype(F32) * c).astype(BF16)
    o_ref[...] = x_ref[...] + _dot(merged, wo_ref[...])


def _merge(x, a, u, ga, gb, dw_w, dw_b, ln_g, ln_b, wa, wc, wo, batch, seq, tm):
    t = x.shape[0]
    tiles = seq // tm
    row = pl.BlockSpec((tm, D_MODEL), lambda b, i: (b * tiles + i, 0))
    vec = _const_spec((1, D_MODEL))
    mat = _const_spec((D_MODEL, D_MODEL))
    dw_tmaj = dw_w.reshape(CONV_WIDTH * SUBLANES, LANES)
    db_tmaj = dw_b.reshape(SUBLANES, LANES)
    return pl.pallas_call(
        functools.partial(_merge_kernel, tm=tm),
        grid=(batch, tiles),
        in_specs=[row, row, row, row, row, _const_spec(dw_tmaj.shape), _const_spec(db_tmaj.shape), vec, vec,
                  mat, mat, mat],
        out_specs=row,
        out_shape=jax.ShapeDtypeStruct((t, D_MODEL), F32),
        scratch_shapes=[pltpu.VMEM(((CONV_HALO + tm) * SUBLANES, LANES), F32),
                        pltpu.VMEM((tm * SUBLANES, LANES), F32)],
        compiler_params=_params(2),
        name="conv_merge",
    )(x, a, u, ga, gb, dw_tmaj, db_tmaj, ln_g, ln_b, wa, wc, wo)


def _rope_tables(seq):
    half = ROPE_DIM // 2
    inv_freq = ROPE_THETA ** (-jnp.arange(0, ROPE_DIM, 2, dtype=F32) / ROPE_DIM)
    d = np.arange(LANES) % HEAD_DIM
    ang = jnp.arange(seq, dtype=F32)[:, None] * inv_freq[d % half][None, :]
    cos, sin = jnp.cos(ang), jnp.sin(ang)
    c_tab = jnp.where(d < ROPE_DIM, cos, 1.0)
    s_fwd = jnp.where(d < half, -sin, 0.0)
    s_bwd = jnp.where((d >= half) & (d < ROPE_DIM), sin, 0.0)
    return c_tab, s_fwd, s_bwd


def _block_ones(width):
    head = np.arange(width) // HEAD_DIM
    return jnp.asarray(head[:, None] == head[None, :], BF16)


def kernel(x, p, ffn1_norm, ffn1_wgu, ffn1_wdown, mix_norm, w_in, q_norm, k_norm, sinks, attn_proj, dw_w, dw_b, conv_ln_g, conv_ln_b, conv_proj, w_out, ffn2_norm, ffn2_wgu, ffn2_wdown, ple_proj, ple_norm, ple_gate_norm, ple_gate_w):
    batch, seq, _ = x.shape
    depth = ffn1_norm.shape[0]
    t = batch * seq
    tm = 512
    xt = x.reshape(t, D_MODEL)
    c_tab, s_fwd, s_bwd = _rope_tables(seq)
    ones_q, ones_k = _block_ones(2 * LANES), _block_ones(KV_W)
    vec = lambda a: a.reshape(1, -1)
    bf = lambda a: a.astype(BF16)
    o_k = ATTN_W
    o_c = o_k + 2 * KV_W
    o_g = o_c + 2 * D_MODEL
    for i in range(depth):
        xt = _ffn(xt, vec(ffn1_norm[i]), bf(ffn1_wgu[i][:, :D_FF]), bf(ffn1_wgu[i][:, D_FF:]),
                  bf(ffn1_wdown[i]), tm)
        wi = w_in[i]
        q, k, v, u, ga, gb = _proj_in(
            xt, vec(mix_norm[i]), bf(wi[:, :o_k]), bf(wi[:, o_k:o_c]), bf(wi[:, o_c:o_g]), bf(wi[:, o_g:]),
            ones_q, ones_k, vec(jnp.tile(q_norm[i], N_Q_HEADS)), vec(jnp.tile(k_norm[i], N_KV_HEADS)),
            c_tab, s_fwd, s_bwd, tm, seq)
        a = _attention(q, k, v, sinks[i], batch, seq, tm)
        xt = _merge(xt, a, u, ga, gb, dw_w[i], dw_b[i], vec(conv_ln_g[i]), vec(conv_ln_b[i]),
                    bf(attn_proj[i]), bf(conv_proj[i]), bf(w_out[i]), batch, seq, tm)
        xt = _ffn_ple(xt, p[i].reshape(t, PLE_DIM), vec(ffn2_norm[i]), bf(ffn2_wgu[i][:, :D_FF]),
                      bf(ffn2_wgu[i][:, D_FF:]), bf(ffn2_wdown[i]), bf(ple_proj[i]), vec(ple_norm[i]),
                      vec(ple_gate_norm[i]), bf(ple_gate_w[i]), tm)
    return xt.reshape(batch, seq, D_MODEL)
```

```python
import functools

import jax
import jax.numpy as jnp
import numpy as np
from jax import lax
from jax.experimental import pallas as pl
from jax.experimental.pallas import tpu as pltpu

D_MODEL = 1024
N_Q_HEADS = 16
N_KV_HEADS = 2
HEAD_DIM = 64
WINDOW = 128
BLOCK = 128
ROPE_THETA = 500000.0
ROPE_DIM = HEAD_DIM // 4
ATTN_W = N_Q_HEADS * HEAD_DIM
KV_W = N_KV_HEADS * HEAD_DIM
CONV_WIDTH = 31
D_FF = 2816
PLE_DIM = 256
EPS = 1e-6
NEG_INF = -1e30

LANES = 128
SUBLANES = 8
assert D_MODEL == SUBLANES * LANES
CONV_HALO = 32
CONV_STEPS = 8
FF_CHUNK = 256
assert D_FF % FF_CHUNK == 0
VMEM_LIMIT = 56 * 1024 * 1024
FRONT_TILE = 256
TILE = 512

BF16 = jnp.bfloat16
F32 = jnp.float32


def _dot(a, b):
    return jnp.dot(a, b, preferred_element_type=F32)


def _rmsnorm(x, g):
    return x * lax.rsqrt(jnp.mean(x * x, axis=-1, keepdims=True) + EPS) * g


def _const_spec(shape):
    return pl.BlockSpec(shape, lambda *_: (0,) * len(shape), pipeline_mode=pl.Buffered(1))


def _params(n_axes, flags=None):
    return pltpu.CompilerParams(dimension_semantics=("arbitrary",) * n_axes,
                                vmem_limit_bytes=VMEM_LIMIT, flags=flags)


def _swiglu_update(x, g_ref, wg_ref, wu_ref, wd_ref):
    h = _rmsnorm(x, g_ref[...]).astype(BF16)
    g = _dot(h, wg_ref[...])
    u = _dot(h, wu_ref[...])
    a = (g * jax.nn.sigmoid(g) * u).astype(BF16)
    return x + 0.5 * _dot(a, wd_ref[...])


def _zero_after(values):
    half_word = jnp.uint32(16)
    bits = functools.reduce(jnp.bitwise_or, [pltpu.bitcast(v, jnp.uint32) for v in values])
    return lax.shift_right_logical(lax.shift_right_logical(bits, half_word), half_word).astype(F32)


def _swiglu_update_chunked(x, g_ref, wg_ref, wu_ref, wd_ref, h_ref, between):
    h_ref[...] = _rmsnorm(x, g_ref[...]).astype(BF16)
    first_tile = (pl.ds(0, 2 * SUBLANES), pl.ds(0, LANES))

    zeros = []

    def after_matmul():
        if zeros:
            z = zeros.pop()
            h_ref[first_tile] = (h_ref[first_tile].astype(F32) + jnp.concatenate([z, z], axis=0)).astype(BF16)
        v = between()
        if v is not None:
            zeros.append(_zero_after(v))

    y = None
    for c in range(0, D_FF, FF_CHUNK):
        g = _dot(h_ref[...], wg_ref[:, c:c + FF_CHUNK])
        after_matmul()
        u = _dot(h_ref[...], wu_ref[:, c:c + FF_CHUNK])
        after_matmul()
        a = (g * jax.nn.sigmoid(g) * u).astype(BF16)
        d = _dot(a, wd_ref[c:c + FF_CHUNK, :])
        after_matmul()
        y = d if y is None else y + d
    return x + 0.5 * y


def _ffn_ple_kernel(x_ref, p_ref, g_ref, wg_ref, wu_ref, wd_ref, pw_ref, pn_ref, gn_ref, gw_ref, o_ref):
    x = _swiglu_update(x_ref[...], g_ref, wg_ref, wu_ref, wd_ref)
    e = _rmsnorm(_dot(p_ref[...].astype(BF16), pw_ref[...]), pn_ref[...])
    gate = jax.nn.sigmoid(_dot(_rmsnorm(x, gn_ref[...]).astype(BF16), gw_ref[...]))
    o_ref[...] = x + gate * e


def _ffn_ple(x, p, norm, wg, wu, wd, ple_w, ple_norm, gate_norm, gate_w, tm):
    t = x.shape[0]
    row = pl.BlockSpec((tm, D_MODEL), lambda i: (i, 0))
    vec = _const_spec((1, D_MODEL))
    return pl.pallas_call(
        _ffn_ple_kernel,
        grid=(t // tm,),
        in_specs=[row, pl.BlockSpec((tm, PLE_DIM), lambda i: (i, 0)), vec,
                  _const_spec((D_MODEL, D_FF)), _const_spec((D_MODEL, D_FF)), _const_spec((D_FF, D_MODEL)),
                  _const_spec((PLE_DIM, D_MODEL)), vec, vec, _const_spec((D_MODEL, D_MODEL))],
        out_specs=row,
        out_shape=jax.ShapeDtypeStruct((t, D_MODEL), F32),
        compiler_params=_params(1),
        name="ffn2_ple",
    )(x, p, norm, wg, wu, wd, ple_w, ple_norm, gate_norm, gate_w)


def _head_norm_rope(z, ones_blk, gain, c_tab, s_fwd, s_bwd):
    w = z.shape[1]
    blk = ones_blk.shape[0]
    sq = (z * z).astype(BF16)
    ss = jnp.concatenate([_dot(sq[:, i:i + blk], ones_blk) for i in range(0, w, blk)], axis=1)
    t = z * lax.rsqrt(ss * (1.0 / HEAD_DIM) + EPS) * gain
    half = ROPE_DIM // 2
    up = pltpu.roll(t, w - half, axis=1)
    dn = pltpu.roll(t, half, axis=1)
    reps = w // LANES
    tile = lambda a: jnp.concatenate([a] * reps, axis=1) if reps > 1 else a
    return t * tile(c_tab) + up * tile(s_fwd) + dn * tile(s_bwd)


def _conv_ln_silu(tmaj_ref, y_ref, dw_ref, db_ref, lg_ref, lb_ref, tm):
    base = CONV_HALO - (CONV_WIDTH - 1)
    pending = list(range(0, tm, CONV_STEPS))

    def emit():
        if not pending:
            return None
        t0 = pending.pop(0)
        acc = [None] * CONV_STEPS
        for j in range(CONV_WIDTH):
            w_j = dw_ref[pl.ds(j * SUBLANES, SUBLANES), :]
            for r in range(CONV_STEPS):
                v = tmaj_ref[pl.ds((t0 + r + base + j) * SUBLANES, SUBLANES), :] * w_j
                acc[r] = v if j == 0 else acc[r] + v
        bias = db_ref[...]
        for r in range(CONV_STEPS):
            y_ref[pl.ds((t0 + r) * SUBLANES, SUBLANES), :] = acc[r] + bias
        return acc

    def finish():
        while pending:
            emit()
        y = jnp.concatenate([y_ref[pl.ds(c, tm, stride=SUBLANES), :] for c in range(SUBLANES)], axis=1)
        mu = jnp.mean(y, axis=-1, keepdims=True)
        yc = y - mu
        var = jnp.mean(yc * yc, axis=-1, keepdims=True)
        ln = yc * lax.rsqrt(var + EPS) * lg_ref[...] + lb_ref[...]
        return (ln * jax.nn.sigmoid(ln)).astype(BF16)

    return emit, finish


def _front_kernel(x_ref, n1_ref, wg_ref, wu_ref, wd_ref, nm_ref, wq_ref, wkv_ref, wc_ref, wgt_ref,
                  oq_ref, ok_ref, gq_ref, gk_ref, c_ref, sf_ref, sb_ref, dw_ref, db_ref, lg_ref, lb_ref,
                  x1_out, q_out, k_out, v_out, ga_out, gb_out, cact_out, tmaj_ref, y_ref, h_ref, *, tm, tiles_per_seq):
    step = pl.program_id(0)
    halo_rows = CONV_HALO * SUBLANES

    @pl.when(step == 0)
    def _():
        tmaj_ref[...] = jnp.zeros(tmaj_ref.shape, F32)

    conv_emit, conv_finish = _conv_ln_silu(tmaj_ref, y_ref, dw_ref, db_ref, lg_ref, lb_ref, tm)
    x1 = _swiglu_update_chunked(x_ref[...], n1_ref, wg_ref, wu_ref, wd_ref, h_ref, conv_emit)
    cact_out[...] = conv_finish()
    x1_out[...] = x1
    h = _rmsnorm(x1, nm_ref[...]).astype(BF16)
    c_tab, s_fwd, s_bwd = c_ref[...], sf_ref[...], sb_ref[...]
    zq = _dot(h, wq_ref[...])
    scale = HEAD_DIM ** -0.5
    q_out[...] = (_head_norm_rope(zq, oq_ref[...], gq_ref[...], c_tab, s_fwd, s_bwd) * scale).astype(BF16)
    zkv = _dot(h, wkv_ref[...])
    k_out[...] = _head_norm_rope(zkv[:, :KV_W], ok_ref[...], gk_ref[...], c_tab, s_fwd, s_bwd).astype(BF16)
    v_out[...] = zkv[:, KV_W:].astype(BF16)
    zg = _dot(h, wgt_ref[...])
    ga_out[...] = jax.nn.sigmoid(zg[:, :D_MODEL]).astype(BF16)
    gb_out[...] = jax.nn.sigmoid(zg[:, D_MODEL:]).astype(BF16)
    zc = _dot(h, wc_ref[...])
    u = zc[:, :D_MODEL] * jax.nn.sigmoid(zc[:, D_MODEL:])

    tile = jnp.minimum(step, pl.num_programs(0) - 2)
    tail = tmaj_ref[pl.ds(tm * SUBLANES, halo_rows), :]
    tmaj_ref[pl.ds(0, halo_rows), :] = jnp.where(tile % tiles_per_seq == 0, jnp.zeros_like(tail), tail)
    for c in range(SUBLANES):
        tmaj_ref[pl.ds(halo_rows + c, tm, stride=SUBLANES), :] = u[:, c * LANES:(c + 1) * LANES]


def _front(x, n1, wg, wu, wd, nm, wq, wkv, wc, wgt, ones_q, ones_k, gq, gk, c_tab, s_fwd, s_bwd,
           dw_w, dw_b, ln_g, ln_b, tm, seq):
    t = x.shape[0]
    n_tiles = t // tm
    tiles_per_seq = seq // tm
    cur = lambda s: jnp.minimum(s, n_tiles - 1)
    row = lambda w: pl.BlockSpec((tm, w), lambda s: (cur(s), 0))
    lag = pl.BlockSpec((tm, D_MODEL), lambda s: (jnp.maximum(s - 1, 0), 0))
    tab = pl.BlockSpec((tm, LANES), lambda s: (cur(s) % tiles_per_seq, 0))
    vec = _const_spec((1, D_MODEL))
    out = lambda w, dt=BF16: jax.ShapeDtypeStruct((t, w), dt)
    dw_tmaj = dw_w.reshape(CONV_WIDTH * SUBLANES, LANES)
    db_tmaj = dw_b.reshape(SUBLANES, LANES)
    return pl.pallas_call(
        functools.partial(_front_kernel, tm=tm, tiles_per_seq=tiles_per_seq),
        grid=(n_tiles + 1,),
        in_specs=[row(D_MODEL), vec, _const_spec(wg.shape), _const_spec(wu.shape), _const_spec(wd.shape), vec,
                  _const_spec(wq.shape), _const_spec(wkv.shape), _const_spec(wc.shape), _const_spec(wgt.shape),
                  _const_spec(ones_q.shape), _const_spec(ones_k.shape),
                  _const_spec((1, ATTN_W)), _const_spec((1, KV_W)), tab, tab, tab,
                  _const_spec(dw_tmaj.shape), _const_spec(db_tmaj.shape), vec, vec],
        out_specs=[row(D_MODEL), row(ATTN_W), row(KV_W), row(KV_W), row(D_MODEL), row(D_MODEL), lag],
        out_shape=[out(D_MODEL, F32), out(ATTN_W), out(KV_W), out(KV_W), out(D_MODEL), out(D_MODEL),
                   out(D_MODEL)],
        scratch_shapes=[pltpu.VMEM(((CONV_HALO + tm) * SUBLANES, LANES), F32),
                        pltpu.VMEM((tm * SUBLANES, LANES), F32),
                        pltpu.VMEM((tm, D_MODEL), BF16)],
        compiler_params=_params(1),
        name="ffn1_proj_conv",
    )(x, n1, wg, wu, wd, nm, wq, wkv, wc, wgt, ones_q, ones_k, gq, gk, c_tab, s_fwd, s_bwd,
      dw_tmaj, db_tmaj, ln_g, ln_b)


def _attn_kernel(sinks_ref, q_ref, k_ref, v_ref, o_ref, *, q_tile):
    tile_start = pl.program_id(1) * q_tile
    lane = lax.broadcasted_iota(jnp.int32, (2 * BLOCK, LANES), 1)
    lo = lane < HEAD_DIM
    tiles_per_kv = ATTN_W // LANES // N_KV_HEADS
    qi = lax.broadcasted_iota(jnp.int32, (tiles_per_kv * BLOCK, 2 * BLOCK), 0) % BLOCK
    kj = lax.broadcasted_iota(jnp.int32, (tiles_per_kv * BLOCK, 2 * BLOCK), 1)
    delta = qi + BLOCK - kj
    band = (delta >= 0) & (delta < WINDOW)
    out_lo = lax.broadcasted_iota(jnp.int32, (tiles_per_kv * BLOCK, LANES), 1) < HEAD_DIM
    key_lo = lax.broadcasted_iota(jnp.int32, (2 * 2 * BLOCK, LANES), 0) < 2 * BLOCK
    ones_hat = (key_lo == out_lo).astype(BF16)

    for blk in range(q_tile // BLOCK):
        start = pl.multiple_of(tile_start + blk * BLOCK, BLOCK)
        prev = pl.multiple_of(jnp.maximum(start - BLOCK, 0), BLOCK)
        valid = band & (start + kj - BLOCK >= 0)
        rows = pl.ds(blk * BLOCK, BLOCK)

        def band_of(ref):
            return jnp.concatenate([ref[pl.ds(prev, BLOCK), :], ref[pl.ds(start, BLOCK), :]], axis=0)

        kf, vf = band_of(k_ref), band_of(v_ref)
        kr, vr = pltpu.roll(kf, HEAD_DIM, axis=1), pltpu.roll(vf, HEAD_DIM, axis=1)
        zero = jnp.zeros_like(kf)
        for h in range(N_KV_HEADS):
            k_a, k_b = (kf, kr) if h == 0 else (kr, kf)
            v_a, v_b = (vf, vr) if h == 0 else (vr, vf)
            k_hat = jnp.concatenate([jnp.where(lo, k_a, zero), jnp.where(lo, zero, k_b)], axis=0)
            v_hat = jnp.concatenate([jnp.where(lo, v_a, zero), jnp.where(lo, zero, v_b)], axis=0)
            q_h = jnp.concatenate(
                [q_ref[rows, pl.ds((h * tiles_per_kv + c) * LANES, LANES)] for c in range(tiles_per_kv)], axis=0)
            s = lax.dot_general(q_h, k_hat, (((1,), (1,)), ((), ())), preferred_element_type=F32)
            probs, sink_terms = [], []
            for half in range(2):
                sh = jnp.where(valid, s[:, half * 2 * BLOCK:(half + 1) * 2 * BLOCK], NEG_INF)
                sink = jnp.concatenate(
                    [jnp.full((BLOCK, LANES), sinks_ref[2 * (h * tiles_per_kv + c) + half], F32)
                     for c in range(tiles_per_kv)], axis=0)
                m = jnp.max(jnp.maximum(jnp.maximum(sh[:, :LANES], sh[:, LANES:]), sink), axis=-1, keepdims=True)
                probs.append(jnp.exp(sh - m).astype(BF16))
                sink_terms.append(jnp.exp(sink - m))
            o_aug = _dot(jnp.concatenate(probs, axis=1), jnp.concatenate([v_hat, ones_hat], axis=1))
            denom = o_aug[:, LANES:] + jnp.where(out_lo, sink_terms[0], sink_terms[1])
            o2 = (o_aug[:, :LANES] / denom).astype(BF16)
            for c in range(tiles_per_kv):
                o_ref[rows, pl.ds((h * tiles_per_kv + c) * LANES, LANES)] = o2[c * BLOCK:(c + 1) * BLOCK]


def _attention(q, k, v, sinks, batch, seq, q_tile):
    t = q.shape[0]
    tiles = seq // q_tile
    return pl.pallas_call(
        functools.partial(_attn_kernel, q_tile=q_tile),
        grid_spec=pltpu.PrefetchScalarGridSpec(
            num_scalar_prefetch=1,
            grid=(batch, tiles),
            in_specs=[pl.BlockSpec((q_tile, ATTN_W), lambda b, i, s: (b * tiles + i, 0)),
                      pl.BlockSpec((seq, KV_W), lambda b, i, s: (b, 0)),
                      pl.BlockSpec((seq, KV_W), lambda b, i, s: (b, 0))],
            out_specs=pl.BlockSpec((q_tile, ATTN_W), lambda b, i, s: (b * tiles + i, 0)),
        ),
        out_shape=jax.ShapeDtypeStruct((t, ATTN_W), BF16),
        compiler_params=_params(2),
        name="swa",
    )(sinks, q, k, v)


def _merge_kernel(x_ref, a_ref, c_ref, ga_ref, gb_ref, wa_ref, wc_ref, wo_ref, o_ref):
    c = _dot(c_ref[...], wc_ref[...])
    a = _dot(a_ref[...], wa_ref[...])
    merged = (ga_ref[...].astype(F32) * a + gb_ref[...].astype(F32) * c).astype(BF16)
    o_ref[...] = x_ref[...] + _dot(merged, wo_ref[...])


def _merge(x, a, c_act, ga, gb, wa, wc, wo, tm):
    t = x.shape[0]
    row = pl.BlockSpec((tm, D_MODEL), lambda i: (i, 0))
    mat = _const_spec((D_MODEL, D_MODEL))
    return pl.pallas_call(
        _merge_kernel,
        grid=(t // tm,),
        in_specs=[row, row, row, row, row, mat, mat, mat],
        out_specs=row,
        out_shape=jax.ShapeDtypeStruct((t, D_MODEL), F32),
        compiler_params=_params(1),
        name="merge",
    )(x, a, c_act, ga, gb, wa, wc, wo)


def _rope_tables(seq):
    half = ROPE_DIM // 2
    inv_freq = ROPE_THETA ** (-jnp.arange(0, ROPE_DIM, 2, dtype=F32) / ROPE_DIM)
    ang = jnp.arange(seq, dtype=F32)[:, None] * inv_freq[None, :]
    cos, sin = jnp.cos(ang), jnp.sin(ang)
    d = np.arange(LANES) % HEAD_DIM
    cos_l, sin_l = jnp.ones((seq, LANES), F32), jnp.zeros((seq, LANES), F32)
    for j in range(half):
        pick = (d < ROPE_DIM) & (d % half == j)
        cos_l = jnp.where(pick, cos[:, j:j + 1], cos_l)
        sin_l = jnp.where(pick, sin[:, j:j + 1], sin_l)
    return cos_l, jnp.where(d < half, -sin_l, 0.0), jnp.where(d >= half, sin_l, 0.0)


def _block_ones(width):
    head = np.arange(width) // HEAD_DIM
    return jnp.asarray(head[:, None] == head[None, :], BF16)


def kernel(x, p, ffn1_norm, ffn1_wgu, ffn1_wdown, mix_norm, w_in, q_norm, k_norm, sinks, attn_proj, dw_w, dw_b, conv_ln_g, conv_ln_b, conv_proj, w_out, ffn2_norm, ffn2_wgu, ffn2_wdown, ple_proj, ple_norm, ple_gate_norm, ple_gate_w):
    batch, seq, _ = x.shape
    depth = ffn1_norm.shape[0]
    t = batch * seq
    xt = x.reshape(t, D_MODEL)
    c_tab, s_fwd, s_bwd = _rope_tables(seq)
    ones_q, ones_k = _block_ones(2 * LANES), _block_ones(KV_W)
    vec = lambda a: a.reshape(1, -1)
    bf = lambda a: a.astype(BF16)
    o_k = ATTN_W
    o_c = o_k + 2 * KV_W
    o_g = o_c + 2 * D_MODEL
    for i in range(depth):
        wi = w_in[i]
        xt, q, k, v, ga, gb, c_act = _front(
            xt, vec(ffn1_norm[i]), bf(ffn1_wgu[i][:, :D_FF]), bf(ffn1_wgu[i][:, D_FF:]), bf(ffn1_wdown[i]),
            vec(mix_norm[i]), bf(wi[:, :o_k]), bf(wi[:, o_k:o_c]), bf(wi[:, o_c:o_g]), bf(wi[:, o_g:]),
            ones_q, ones_k, vec(jnp.tile(q_norm[i], N_Q_HEADS)), vec(jnp.tile(k_norm[i], N_KV_HEADS)),
            c_tab, s_fwd, s_bwd, dw_w[i], dw_b[i], vec(conv_ln_g[i]), vec(conv_ln_b[i]), FRONT_TILE, seq)
        a = _attention(q, k, v, sinks[i], batch, seq, TILE)
        xt = _merge(xt, a, c_act, ga, gb, bf(attn_proj[i]), bf(conv_proj[i]), bf(w_out[i]), TILE)
        xt = _ffn_ple(xt, p[i].reshape(t, PLE_DIM), vec(ffn2_norm[i]), bf(ffn2_wgu[i][:, :D_FF]),
                      bf(ffn2_wgu[i][:, D_FF:]), bf(ffn2_wdown[i]), bf(ple_proj[i]), vec(ple_norm[i]),
                      vec(ple_gate_norm[i]), bf(ple_gate_w[i]), TILE)
    return xt.reshape(batch, seq, D_MODEL)
```

```python
import functools

import jax
import jax.numpy as jnp
import numpy as np
from jax import lax
from jax.experimental import pallas as pl
from jax.experimental.pallas import tpu as pltpu

D_MODEL = 1024
N_Q_HEADS = 16
N_KV_HEADS = 2
HEAD_DIM = 64
WINDOW = 128
BLOCK = 128
ROPE_THETA = 500000.0
ROPE_DIM = HEAD_DIM // 4
ATTN_W = N_Q_HEADS * HEAD_DIM
KV_W = N_KV_HEADS * HEAD_DIM
CONV_WIDTH = 31
D_FF = 2816
PLE_DIM = 256
EPS = 1e-6
NEG_INF = -1e30

LANES = 128
SUBLANES = 8
assert D_MODEL == SUBLANES * LANES
CONV_HALO = 32
CONV_STEPS = 8
LN_ROWS = 32
FF_CHUNK = 256
assert D_FF % FF_CHUNK == 0
VMEM_LIMIT = 60 * 1024 * 1024
FRONT_TILE = 512
TILE = 512

BF16 = jnp.bfloat16
F32 = jnp.float32


def _dot(a, b):
    return jnp.dot(a, b, preferred_element_type=F32)


def _rmsnorm(x, g):
    return x * lax.rsqrt(jnp.mean(x * x, axis=-1, keepdims=True) + EPS) * g


def _const_spec(shape):
    return pl.BlockSpec(shape, lambda *_: (0,) * len(shape), pipeline_mode=pl.Buffered(1))


def _params(n_axes, flags=None):
    return pltpu.CompilerParams(dimension_semantics=("arbitrary",) * n_axes,
                                vmem_limit_bytes=VMEM_LIMIT, flags=flags)


def _swiglu_update(x, g_ref, wg_ref, wu_ref, wd_ref):
    h = _rmsnorm(x, g_ref[...]).astype(BF16)
    g = _dot(h, wg_ref[...])
    u = _dot(h, wu_ref[...])
    a = (g * jax.nn.sigmoid(g) * u).astype(BF16)
    return x + 0.5 * _dot(a, wd_ref[...])


def _zero_after(values):
    tiles = [v[r:r + SUBLANES, c:c + LANES] for v in values
             for r in range(0, v.shape[0], SUBLANES) for c in range(0, v.shape[1], LANES)]
    half_word = jnp.uint32(16)
    bits = functools.reduce(jnp.bitwise_or, [pltpu.bitcast(t, jnp.uint32) for t in tiles])
    return lax.shift_right_logical(lax.shift_right_logical(bits, half_word), half_word).astype(F32)


class _Weaver:
    def __init__(self, lhs_ref, piece, pin_start=False):
        self.lhs_ref, self.piece, self.zero, self.start, self.pin_start = lhs_ref, piece, None, None, pin_start

    def after_matmul(self, result, n=1):
        tile = (pl.ds(0, 2 * SUBLANES), pl.ds(0, LANES))
        if self.zero is not None:
            z = jnp.concatenate([self.zero, self.zero], axis=0)
            self.lhs_ref[tile] = (self.lhs_ref[tile].astype(F32) + z).astype(BF16)
        results = []
        for _ in range(n):
            results.extend(self.piece(self.start) or [])
        self.zero = _zero_after(results) if results else None
        self.start = _zero_after([result[:SUBLANES, :LANES]]) if self.pin_start else None


def _swiglu_update_chunked(x, g_ref, wg_ref, wu_ref, wd_ref, weaver, n_pieces):
    h_ref = weaver.lhs_ref
    h_ref[...] = _rmsnorm(x, g_ref[...]).astype(BF16)
    y = None
    for c in range(0, D_FF, FF_CHUNK):
        g = _dot(h_ref[...], wg_ref[:, c:c + FF_CHUNK])
        weaver.after_matmul(g, n_pieces)
        u = _dot(h_ref[...], wu_ref[:, c:c + FF_CHUNK])
        weaver.after_matmul(u, n_pieces)
        a = (g * jax.nn.sigmoid(g) * u).astype(BF16)
        d = _dot(a, wd_ref[c:c + FF_CHUNK, :])
        weaver.after_matmul(d, n_pieces)
        y = d if y is None else y + d
    return x + 0.5 * y


def _ffn_ple_kernel(x_ref, p_ref, g_ref, wg_ref, wu_ref, wd_ref, pw_ref, pn_ref, gn_ref, gw_ref, o_ref):
    x = _swiglu_update(x_ref[...], g_ref, wg_ref, wu_ref, wd_ref)
    e = _rmsnorm(_dot(p_ref[...].astype(BF16), pw_ref[...]), pn_ref[...])
    gate = jax.nn.sigmoid(_dot(_rmsnorm(x, gn_ref[...]).astype(BF16), gw_ref[...]))
    o_ref[...] = x + gate * e


def _ffn_ple(x, p, norm, wg, wu, wd, ple_w, ple_norm, gate_norm, gate_w, tm):
    t = x.shape[0]
    row = pl.BlockSpec((tm, D_MODEL), lambda i: (i, 0))
    vec = _const_spec((1, D_MODEL))
    return pl.pallas_call(
        _ffn_ple_kernel,
        grid=(t // tm,),
        in_specs=[row, pl.BlockSpec((tm, PLE_DIM), lambda i: (i, 0)), vec,
                  _const_spec((D_MODEL, D_FF)), _const_spec((D_MODEL, D_FF)), _const_spec((D_FF, D_MODEL)),
                  _const_spec((PLE_DIM, D_MODEL)), vec, vec, _const_spec((D_MODEL, D_MODEL))],
        out_specs=row,
        out_shape=jax.ShapeDtypeStruct((t, D_MODEL), F32),
        compiler_params=_params(1),
        name="ffn2_ple",
    )(x, p, norm, wg, wu, wd, ple_w, ple_norm, gate_norm, gate_w)


def _head_norm_rope(z, ones_blk, gain, c_tab, s_fwd, s_bwd):
    w = z.shape[1]
    blk = ones_blk.shape[0]
    sq = (z * z).astype(BF16)
    ss = jnp.concatenate([_dot(sq[:, i:i + blk], ones_blk) for i in range(0, w, blk)], axis=1)
    t = z * lax.rsqrt(ss * (1.0 / HEAD_DIM) + EPS) * gain
    half = ROPE_DIM // 2
    up = pltpu.roll(t, w - half, axis=1)
    dn = pltpu.roll(t, half, axis=1)
    reps = w // LANES
    tile = lambda a: jnp.concatenate([a] * reps, axis=1) if reps > 1 else a
    return t * tile(c_tab) + up * tile(s_fwd) + dn * tile(s_bwd)


def _conv_ln_silu(tmaj_ref, y_ref, dw_ref, db_ref, lg_ref, lb_ref, out_ref, tm):
    base = CONV_HALO - (CONV_WIDTH - 1)

    def conv_piece(t0, start):
        acc = [None] * CONV_STEPS
        for j in range(CONV_WIDTH):
            w_j = dw_ref[pl.ds(j * SUBLANES, SUBLANES), :]
            if j == 0 and start is not None:
                w_j = w_j + start
            for r in range(CONV_STEPS):
                v = tmaj_ref[pl.ds((t0 + r + base + j) * SUBLANES, SUBLANES), :] * w_j
                acc[r] = v if j == 0 else acc[r] + v
        bias = db_ref[...]
        acc = [a + bias for a in acc]
        for r in range(CONV_STEPS):
            y_ref[pl.ds((t0 + r) * SUBLANES, SUBLANES), :] = acc[r]
        return acc

    def ln_piece(t0, start):
        y = jnp.concatenate([y_ref[pl.ds(t0 * SUBLANES + c, LN_ROWS, stride=SUBLANES), :]
                             for c in range(SUBLANES)], axis=1)
        if start is not None:
            y = y + jnp.concatenate([jnp.concatenate([start] * SUBLANES, axis=1)] * (LN_ROWS // SUBLANES), axis=0)
        yc = y - jnp.mean(y, axis=-1, keepdims=True)
        var = jnp.mean(yc * yc, axis=-1, keepdims=True)
        ln = yc * lax.rsqrt(var + EPS) * lg_ref[...] + lb_ref[...]
        act = ln * jax.nn.sigmoid(ln)
        out_ref[pl.ds(t0, LN_ROWS), :] = act.astype(BF16)
        return [act]

    pending = [functools.partial(conv_piece, t0) for t0 in range(0, tm, CONV_STEPS)]
    pending += [functools.partial(ln_piece, t0) for t0 in range(0, tm, LN_ROWS)]

    def piece(start):
        return pending.pop(0)(start) if pending else None

    def finish(start):
        while pending:
            piece(start)

    return piece, finish


def _front_kernel(x_ref, n1_ref, wg_ref, wu_ref, wd_ref, nm_ref, wq_ref, wkv_ref, wc_ref, wgt_ref,
                  oq_ref, ok_ref, gq_ref, gk_ref, c_ref, sf_ref, sb_ref, dw_ref, db_ref, lg_ref, lb_ref,
                  x1_out, q_out, k_out, v_out, ga_out, gb_out, cact_out, tmaj_ref, y_ref, h_ref, *, tm, tiles_per_seq):
    step = pl.program_id(0)
    halo_rows = CONV_HALO * SUBLANES

    @pl.when(step == 0)
    def _():
        tmaj_ref[...] = jnp.zeros(tmaj_ref.shape, F32)

    conv_piece, conv_finish = _conv_ln_silu(tmaj_ref, y_ref, dw_ref, db_ref, lg_ref, lb_ref, cact_out, tm)
    weaver = _Weaver(h_ref, conv_piece)
    ffn_matmuls = 3 * (D_FF // FF_CHUNK)
    x1 = _swiglu_update_chunked(x_ref[...], n1_ref, wg_ref, wu_ref, wd_ref, weaver,
                                pl.cdiv(tm // CONV_STEPS, ffn_matmuls))
    k = tm // (8 * LN_ROWS)
    x1_out[...] = x1
    h_ref[...] = _rmsnorm(x1, nm_ref[...]).astype(BF16)
    c_tab, s_fwd, s_bwd = c_ref[...], sf_ref[...], sb_ref[...]
    zq = _dot(h_ref[...], wq_ref[...])
    weaver.after_matmul(zq, k)
    scale = HEAD_DIM ** -0.5
    q_out[...] = (_head_norm_rope(zq, oq_ref[...], gq_ref[...], c_tab, s_fwd, s_bwd) * scale).astype(BF16)
    zkv = _dot(h_ref[...], wkv_ref[...])
    weaver.after_matmul(zkv, k)
    k_out[...] = _head_norm_rope(zkv[:, :KV_W], ok_ref[...], gk_ref[...], c_tab, s_fwd, s_bwd).astype(BF16)
    v_out[...] = zkv[:, KV_W:].astype(BF16)
    za = _dot(h_ref[...], wgt_ref[:, :D_MODEL])
    weaver.after_matmul(za, 2 * k)
    ga_out[...] = jax.nn.sigmoid(za).astype(BF16)
    zb = _dot(h_ref[...], wgt_ref[:, D_MODEL:])
    weaver.after_matmul(zb, 2 * k)
    gb_out[...] = jax.nn.sigmoid(zb).astype(BF16)
    u_lin = _dot(h_ref[...], wc_ref[:, :D_MODEL])
    weaver.after_matmul(u_lin, 2 * k)
    conv_finish(weaver.start)
    u = u_lin * jax.nn.sigmoid(_dot(h_ref[...], wc_ref[:, D_MODEL:]))

    tile = jnp.minimum(step, pl.num_programs(0) - 2)
    tail = tmaj_ref[pl.ds(tm * SUBLANES, halo_rows), :]
    tmaj_ref[pl.ds(0, halo_rows), :] = jnp.where(tile % tiles_per_seq == 0, jnp.zeros_like(tail), tail)
    for c in range(SUBLANES):
        tmaj_ref[pl.ds(halo_rows + c, tm, stride=SUBLANES), :] = u[:, c * LANES:(c + 1) * LANES]


def _front(x, n1, wg, wu, wd, nm, wq, wkv, wc, wgt, ones_q, ones_k, gq, gk, c_tab, s_fwd, s_bwd,
           dw_w, dw_b, ln_g, ln_b, tm, seq):
    t = x.shape[0]
    n_tiles = t // tm
    tiles_per_seq = seq // tm
    cur = lambda s: jnp.minimum(s, n_tiles - 1)
    row = lambda w: pl.BlockSpec((tm, w), lambda s: (cur(s), 0))
    lag = pl.BlockSpec((tm, D_MODEL), lambda s: (jnp.maximum(s - 1, 0), 0))
    tab = pl.BlockSpec((tm, LANES), lambda s: (cur(s) % tiles_per_seq, 0))
    vec = _const_spec((1, D_MODEL))
    out = lambda w, dt=BF16: jax.ShapeDtypeStruct((t, w), dt)
    dw_tmaj = dw_w.reshape(CONV_WIDTH * SUBLANES, LANES)
    db_tmaj = dw_b.reshape(SUBLANES, LANES)
    return pl.pallas_call(
        functools.partial(_front_kernel, tm=tm, tiles_per_seq=tiles_per_seq),
        grid=(n_tiles + 1,),
        in_specs=[row(D_MODEL), vec, _const_spec(wg.shape), _const_spec(wu.shape), _const_spec(wd.shape), vec,
                  _const_spec(wq.shape), _const_spec(wkv.shape), _const_spec(wc.shape), _const_spec(wgt.shape),
                  _const_spec(ones_q.shape), _const_spec(ones_k.shape),
                  _const_spec((1, ATTN_W)), _const_spec((1, KV_W)), tab, tab, tab,
                  _const_spec(dw_tmaj.shape), _const_spec(db_tmaj.shape), vec, vec],
        out_specs=[row(D_MODEL), row(ATTN_W), row(KV_W), row(KV_W), row(D_MODEL), row(D_MODEL), lag],
        out_shape=[out(D_MODEL, F32), out(ATTN_W), out(KV_W), out(KV_W), out(D_MODEL), out(D_MODEL),
                   out(D_MODEL)],
        scratch_shapes=[pltpu.VMEM(((CONV_HALO + tm) * SUBLANES, LANES), F32),
                        pltpu.VMEM((tm * SUBLANES, LANES), F32),
                        pltpu.VMEM((tm, D_MODEL), BF16)],
        compiler_params=_params(1),
        name="ffn1_proj_conv",
    )(x, n1, wg, wu, wd, nm, wq, wkv, wc, wgt, ones_q, ones_k, gq, gk, c_tab, s_fwd, s_bwd,
      dw_tmaj, db_tmaj, ln_g, ln_b)


def _attn_kernel(sinks_ref, q_ref, k_ref, v_ref, o_ref, *, q_tile):
    tile_start = pl.program_id(1) * q_tile
    lane = lax.broadcasted_iota(jnp.int32, (2 * BLOCK, LANES), 1)
    lo = lane < HEAD_DIM
    tiles_per_kv = ATTN_W // LANES // N_KV_HEADS
    qi = lax.broadcasted_iota(jnp.int32, (tiles_per_kv * BLOCK, 2 * BLOCK), 0) % BLOCK
    kj = lax.broadcasted_iota(jnp.int32, (tiles_per_kv * BLOCK, 2 * BLOCK), 1)
    delta = qi + BLOCK - kj
    band = (delta >= 0) & (delta < WINDOW)
    out_lo = lax.broadcasted_iota(jnp.int32, (tiles_per_kv * BLOCK, LANES), 1) < HEAD_DIM
    key_lo = lax.broadcasted_iota(jnp.int32, (2 * 2 * BLOCK, LANES), 0) < 2 * BLOCK
    ones_hat = (key_lo == out_lo).astype(BF16)

    for blk in range(q_tile // BLOCK):
        start = pl.multiple_of(tile_start + blk * BLOCK, BLOCK)
        prev = pl.multiple_of(jnp.maximum(start - BLOCK, 0), BLOCK)
        valid = band & (start + kj - BLOCK >= 0)
        rows = pl.ds(blk * BLOCK, BLOCK)

        def band_of(ref):
            return jnp.concatenate([ref[pl.ds(prev, BLOCK), :], ref[pl.ds(start, BLOCK), :]], axis=0)

        kf, vf = band_of(k_ref), band_of(v_ref)
        kr, vr = pltpu.roll(kf, HEAD_DIM, axis=1), pltpu.roll(vf, HEAD_DIM, axis=1)
        zero = jnp.zeros_like(kf)
        for h in range(N_KV_HEADS):
            k_a, k_b = (kf, kr) if h == 0 else (kr, kf)
            v_a, v_b = (vf, vr) if h == 0 else (vr, vf)
            k_hat = jnp.concatenate([jnp.where(lo, k_a, zero), jnp.where(lo, zero, k_b)], axis=0)
            v_hat = jnp.concatenate([jnp.where(lo, v_a, zero), jnp.where(lo, zero, v_b)], axis=0)
            q_h = jnp.concatenate(
                [q_ref[rows, pl.ds((h * tiles_per_kv + c) * LANES, LANES)] for c in range(tiles_per_kv)], axis=0)
            s = lax.dot_general(q_h, k_hat, (((1,), (1,)), ((), ())), preferred_element_type=F32)
            probs, sink_terms = [], []
            for half in range(2):
                sh = jnp.where(valid, s[:, half * 2 * BLOCK:(half + 1) * 2 * BLOCK], NEG_INF)
                sink = jnp.concatenate(
                    [jnp.full((BLOCK, LANES), sinks_ref[2 * (h * tiles_per_kv + c) + half], F32)
                     for c in range(tiles_per_kv)], axis=0)
                m = jnp.max(jnp.maximum(jnp.maximum(sh[:, :LANES], sh[:, LANES:]), sink), axis=-1, keepdims=True)
                probs.append(jnp.exp(sh - m).astype(BF16))
                sink_terms.append(jnp.exp(sink - m))
            o_aug = _dot(jnp.concatenate(probs, axis=1), jnp.concatenate([v_hat, ones_hat], axis=1))
            denom = o_aug[:, LANES:] + jnp.where(out_lo, sink_terms[0], sink_terms[1])
            o2 = (o_aug[:, :LANES] / denom).astype(BF16)
            for c in range(tiles_per_kv):
                o_ref[rows, pl.ds((h * tiles_per_kv + c) * LANES, LANES)] = o2[c * BLOCK:(c + 1) * BLOCK]


def _attention(q, k, v, sinks, batch, seq, q_tile):
    t = q.shape[0]
    tiles = seq // q_tile
    return pl.pallas_call(
        functools.partial(_attn_kernel, q_tile=q_tile),
        grid_spec=pltpu.PrefetchScalarGridSpec(
            num_scalar_prefetch=1,
            grid=(batch, tiles),
            in_specs=[pl.BlockSpec((q_tile, ATTN_W), lambda b, i, s: (b * tiles + i, 0)),
                      pl.BlockSpec((seq, KV_W), lambda b, i, s: (b, 0)),
                      pl.BlockSpec((seq, KV_W), lambda b, i, s: (b, 0))],
            out_specs=pl.BlockSpec((q_tile, ATTN_W), lambda b, i, s: (b * tiles + i, 0)),
        ),
        out_shape=jax.ShapeDtypeStruct((t, ATTN_W), BF16),
        compiler_params=_params(2),
        name="swa",
    )(sinks, q, k, v)


def _merge_kernel(x_ref, a_ref, c_ref, ga_ref, gb_ref, wa_ref, wc_ref, wo_ref, o_ref):
    c = _dot(c_ref[...], wc_ref[...])
    a = _dot(a_ref[...], wa_ref[...])
    merged = (ga_ref[...].astype(F32) * a + gb_ref[...].astype(F32) * c).astype(BF16)
    o_ref[...] = x_ref[...] + _dot(merged, wo_ref[...])


def _merge(x, a, c_act, ga, gb, wa, wc, wo, tm):
    t = x.shape[0]
    row = pl.BlockSpec((tm, D_MODEL), lambda i: (i, 0))
    mat = _const_spec((D_MODEL, D_MODEL))
    return pl.pallas_call(
        _merge_kernel,
        grid=(t // tm,),
        in_specs=[row, row, row, row, row, mat, mat, mat],
        out_specs=row,
        out_shape=jax.ShapeDtypeStruct((t, D_MODEL), F32),
        compiler_params=_params(1),
        name="merge",
    )(x, a, c_act, ga, gb, wa, wc, wo)


def _rope_tables(seq):
    half = ROPE_DIM // 2
    inv_freq = ROPE_THETA ** (-jnp.arange(0, ROPE_DIM, 2, dtype=F32) / ROPE_DIM)
    ang = jnp.arange(seq, dtype=F32)[:, None] * inv_freq[None, :]
    cos, sin = jnp.cos(ang), jnp.sin(ang)
    d = np.arange(LANES) % HEAD_DIM
    cos_l, sin_l = jnp.ones((seq, LANES), F32), jnp.zeros((seq, LANES), F32)
    for j in range(half):
        pick = (d < ROPE_DIM) & (d % half == j)
        cos_l = jnp.where(pick, cos[:, j:j + 1], cos_l)
        sin_l = jnp.where(pick, sin[:, j:j + 1], sin_l)
    return cos_l, jnp.where(d < half, -sin_l, 0.0), jnp.where(d >= half, sin_l, 0.0)


def _block_ones(width):
    head = np.arange(width) // HEAD_DIM
    return jnp.asarray(head[:, None] == head[None, :], BF16)


def kernel(x, p, ffn1_norm, ffn1_wgu, ffn1_wdown, mix_norm, w_in, q_norm, k_norm, sinks, attn_proj, dw_w, dw_b, conv_ln_g, conv_ln_b, conv_proj, w_out, ffn2_norm, ffn2_wgu, ffn2_wdown, ple_proj, ple_norm, ple_gate_norm, ple_gate_w):
    batch, seq, _ = x.shape
    depth = ffn1_norm.shape[0]
    t = batch * seq
    xt = x.reshape(t, D_MODEL)
    c_tab, s_fwd, s_bwd = _rope_tables(seq)
    ones_q, ones_k = _block_ones(2 * LANES), _block_ones(KV_W)
    vec = lambda a: a.reshape(1, -1)
    bf = lambda a: a.astype(BF16)
    o_k = ATTN_W
    o_c = o_k + 2 * KV_W
    o_g = o_c + 2 * D_MODEL
    for i in range(depth):
        wi = w_in[i]
        xt, q, k, v, ga, gb, c_act = _front(
            xt, vec(ffn1_norm[i]), bf(ffn1_wgu[i][:, :D_FF]), bf(ffn1_wgu[i][:, D_FF:]), bf(ffn1_wdown[i]),
            vec(mix_norm[i]), bf(wi[:, :o_k]), bf(wi[:, o_k:o_c]), bf(wi[:, o_c:o_g]), bf(wi[:, o_g:]),
            ones_q, ones_k, vec(jnp.tile(q_norm[i], N_Q_HEADS)), vec(jnp.tile(k_norm[i], N_KV_HEADS)),
            c_tab, s_fwd, s_bwd, dw_w[i], dw_b[i], vec(conv_ln_g[i]), vec(conv_ln_b[i]), FRONT_TILE, seq)
        a = _attention(q, k, v, sinks[i], batch, seq, TILE)
        xt = _merge(xt, a, c_act, ga, gb, bf(attn_proj[i]), bf(conv_proj[i]), bf(w_out[i]), TILE)
        xt = _ffn_ple(xt, p[i].reshape(t, PLE_DIM), vec(ffn2_norm[i]), bf(ffn2_wgu[i][:, :D_FF]),
                      bf(ffn2_wgu[i][:, D_FF:]), bf(ffn2_wdown[i]), bf(ple_proj[i]), vec(ple_norm[i]),
                      vec(ple_gate_norm[i]), bf(ple_gate_w[i]), TILE)
    return xt.reshape(batch, seq, D_MODEL)
```

```python
import functools

import jax
import jax.numpy as jnp
import numpy as np
from jax import lax
from jax.experimental import pallas as pl
from jax.experimental.pallas import tpu as pltpu

D_MODEL = 1024
N_Q_HEADS = 16
N_KV_HEADS = 2
HEAD_DIM = 64
WINDOW = 128
BLOCK = 128
ROPE_THETA = 500000.0
ROPE_DIM = HEAD_DIM // 4
ATTN_W = N_Q_HEADS * HEAD_DIM
KV_W = N_KV_HEADS * HEAD_DIM
CONV_WIDTH = 31
D_FF = 2816
PLE_DIM = 256
EPS = 1e-6
NEG_INF = -1e30
LOG2E = 1.4426950408889634

LANES = 128
SUBLANES = 8
assert D_MODEL == SUBLANES * LANES
CONV_HALO = 32
CONV_STEPS = 8
LN_ROWS = 32
FF_CHUNK = 256
assert D_FF % FF_CHUNK == 0
VMEM_LIMIT = 60 * 1024 * 1024
FRONT_TILE = 512
TILE = 512

BF16 = jnp.bfloat16
F32 = jnp.float32


def _dot(a, b):
    return jnp.dot(a, b, preferred_element_type=F32)


def _rmsnorm(x, g):
    return x * lax.rsqrt(jnp.mean(x * x, axis=-1, keepdims=True) + EPS) * g


def _const_spec(shape):
    return pl.BlockSpec(shape, lambda *_: (0,) * len(shape), pipeline_mode=pl.Buffered(1))


def _params(n_axes, flags=None):
    return pltpu.CompilerParams(dimension_semantics=("arbitrary",) * n_axes,
                                vmem_limit_bytes=VMEM_LIMIT, flags=flags)


def _swiglu_update(x, g_ref, wg_ref, wu_ref, wd_ref):
    h = _rmsnorm(x, g_ref[...]).astype(BF16)
    g = _dot(h, wg_ref[...])
    u = _dot(h, wu_ref[...])
    a = (g * jax.nn.sigmoid(g) * u).astype(BF16)
    return x + 0.5 * _dot(a, wd_ref[...])


def _zero_after(values):
    tiles = [v[r:r + SUBLANES, c:c + LANES] for v in values
             for r in range(0, v.shape[0], SUBLANES) for c in range(0, v.shape[1], LANES)]
    half_word = jnp.uint32(16)
    bits = functools.reduce(jnp.bitwise_or, [pltpu.bitcast(t, jnp.uint32) for t in tiles])
    return lax.shift_right_logical(lax.shift_right_logical(bits, half_word), half_word).astype(F32)


class _Weaver:
    def __init__(self, lhs_ref, piece):
        self.lhs_ref, self.piece, self.zero = lhs_ref, piece, None

    def after_matmul(self, n=1):
        tile = (pl.ds(0, 2 * SUBLANES), pl.ds(0, LANES))
        if self.zero is not None:
            z = jnp.concatenate([self.zero, self.zero], axis=0)
            self.lhs_ref[tile] = (self.lhs_ref[tile].astype(F32) + z).astype(BF16)
        results = []
        for _ in range(n):
            results.extend(self.piece() or [])
        self.zero = _zero_after(results) if results else None


def _swiglu_update_chunked(x, g_ref, wg_ref, wu_ref, wd_ref, weaver, n_pieces):
    h_ref = weaver.lhs_ref
    h_ref[...] = _rmsnorm(x, g_ref[...]).astype(BF16)
    y = None
    for c in range(0, D_FF, FF_CHUNK):
        g = _dot(h_ref[...], wg_ref[:, c:c + FF_CHUNK])
        weaver.after_matmul(n_pieces)
        u = _dot(h_ref[...], wu_ref[:, c:c + FF_CHUNK])
        weaver.after_matmul(n_pieces)
        a = (g * jax.nn.sigmoid(g) * u).astype(BF16)
        d = _dot(a, wd_ref[c:c + FF_CHUNK, :])
        weaver.after_matmul(n_pieces)
        y = d if y is None else y + d
    return x + 0.5 * y


def _ffn_ple_kernel(x_ref, p_ref, g_ref, wg_ref, wu_ref, wd_ref, pw_ref, pn_ref, gn_ref, gw_ref, o_ref):
    x = _swiglu_update(x_ref[...], g_ref, wg_ref, wu_ref, wd_ref)
    e = _rmsnorm(_dot(p_ref[...].astype(BF16), pw_ref[...]), pn_ref[...])
    gate = jax.nn.sigmoid(_dot(_rmsnorm(x, gn_ref[...]).astype(BF16), gw_ref[...]))
    o_ref[...] = x + gate * e


def _ffn_ple(x, p, norm, wg, wu, wd, ple_w, ple_norm, gate_norm, gate_w, tm):
    t = x.shape[0]
    row = pl.BlockSpec((tm, D_MODEL), lambda i: (i, 0))
    vec = _const_spec((1, D_MODEL))
    return pl.pallas_call(
        _ffn_ple_kernel,
        grid=(t // tm,),
        in_specs=[row, pl.BlockSpec((tm, PLE_DIM), lambda i: (i, 0)), vec,
                  _const_spec((D_MODEL, D_FF)), _const_spec((D_MODEL, D_FF)), _const_spec((D_FF, D_MODEL)),
                  _const_spec((PLE_DIM, D_MODEL)), vec, vec, _const_spec((D_MODEL, D_MODEL))],
        out_specs=row,
        out_shape=jax.ShapeDtypeStruct((t, D_MODEL), F32),
        compiler_params=_params(1),
        name="ffn2_ple",
    )(x, p, norm, wg, wu, wd, ple_w, ple_norm, gate_norm, gate_w)


def _head_norm_rope(z, ones_blk, gain, c_tab, s_fwd, s_bwd):
    w = z.shape[1]
    blk = ones_blk.shape[0]
    sq = (z * z).astype(BF16)
    ss = jnp.concatenate([_dot(sq[:, i:i + blk], ones_blk) for i in range(0, w, blk)], axis=1)
    t = z * lax.rsqrt(ss * (1.0 / HEAD_DIM) + EPS) * gain
    half = ROPE_DIM // 2
    up = pltpu.roll(t, w - half, axis=1)
    dn = pltpu.roll(t, half, axis=1)
    reps = w // LANES
    tile = lambda a: jnp.concatenate([a] * reps, axis=1) if reps > 1 else a
    return t * tile(c_tab) + up * tile(s_fwd) + dn * tile(s_bwd)


def _conv_ln_silu(tmaj_ref, y_ref, dw_ref, db_ref, lg_ref, lb_ref, out_ref, tm):
    base = CONV_HALO - (CONV_WIDTH - 1)

    def conv_piece(t0):
        acc = [None] * CONV_STEPS
        for j in range(CONV_WIDTH):
            w_j = dw_ref[pl.ds(j * SUBLANES, SUBLANES), :]
            for r in range(CONV_STEPS):
                v = tmaj_ref[pl.ds((t0 + r + base + j) * SUBLANES, SUBLANES), :] * w_j
                acc[r] = v if j == 0 else acc[r] + v
        bias = db_ref[...]
        acc = [a + bias for a in acc]
        for r in range(CONV_STEPS):
            y_ref[pl.ds((t0 + r) * SUBLANES, SUBLANES), :] = acc[r]
        return acc

    def ln_piece(t0):
        y = jnp.concatenate([y_ref[pl.ds(t0 * SUBLANES + c, LN_ROWS, stride=SUBLANES), :]
                             for c in range(SUBLANES)], axis=1)
        yc = y - jnp.mean(y, axis=-1, keepdims=True)
        var = jnp.mean(yc * yc, axis=-1, keepdims=True)
        ln = yc * lax.rsqrt(var + EPS) * lg_ref[...] + lb_ref[...]
        act = ln * jax.nn.sigmoid(ln)
        out_ref[pl.ds(t0, LN_ROWS), :] = act.astype(BF16)
        return [act]

    pending = [functools.partial(conv_piece, t0) for t0 in range(0, tm, CONV_STEPS)]
    pending += [functools.partial(ln_piece, t0) for t0 in range(0, tm, LN_ROWS)]

    def piece():
        return pending.pop(0)() if pending else None

    def finish():
        while pending:
            piece()

    return piece, finish


def _front_kernel(x_ref, n1_ref, wg_ref, wu_ref, wd_ref, nm_ref, wq_ref, wkv_ref, wc_ref, wgt_ref,
                  oq_ref, ok_ref, gq_ref, gk_ref, c_ref, sf_ref, sb_ref, dw_ref, db_ref, lg_ref, lb_ref,
                  x1_out, q_out, k_out, v_out, ga_out, gb_out, cact_out, tmaj_ref, y_ref, h_ref, *, tm, tiles_per_seq):
    step = pl.program_id(0)
    halo_rows = CONV_HALO * SUBLANES

    @pl.when(step == 0)
    def _():
        tmaj_ref[...] = jnp.zeros(tmaj_ref.shape, F32)

    conv_piece, conv_finish = _conv_ln_silu(tmaj_ref, y_ref, dw_ref, db_ref, lg_ref, lb_ref, cact_out, tm)
    weaver = _Weaver(h_ref, conv_piece)
    ffn_matmuls = 3 * (D_FF // FF_CHUNK)
    x1 = _swiglu_update_chunked(x_ref[...], n1_ref, wg_ref, wu_ref, wd_ref, weaver,
                                pl.cdiv(tm // CONV_STEPS, ffn_matmuls))
    k = tm // (8 * LN_ROWS)
    x1_out[...] = x1
    h_ref[...] = _rmsnorm(x1, nm_ref[...]).astype(BF16)
    c_tab, s_fwd, s_bwd = c_ref[...], sf_ref[...], sb_ref[...]
    zq = _dot(h_ref[...], wq_ref[...])
    weaver.after_matmul(k)
    scale = HEAD_DIM ** -0.5 * LOG2E
    q_out[...] = (_head_norm_rope(zq, oq_ref[...], gq_ref[...], c_tab, s_fwd, s_bwd) * scale).astype(BF16)
    zkv = _dot(h_ref[...], wkv_ref[...])
    weaver.after_matmul(k)
    k_out[...] = _head_norm_rope(zkv[:, :KV_W], ok_ref[...], gk_ref[...], c_tab, s_fwd, s_bwd).astype(BF16)
    v_out[...] = zkv[:, KV_W:].astype(BF16)
    za = _dot(h_ref[...], wgt_ref[:, :D_MODEL])
    weaver.after_matmul(2 * k)
    ga_out[...] = jax.nn.sigmoid(za).astype(BF16)
    zb = _dot(h_ref[...], wgt_ref[:, D_MODEL:])
    weaver.after_matmul(2 * k)
    gb_out[...] = jax.nn.sigmoid(zb).astype(BF16)
    u_lin = _dot(h_ref[...], wc_ref[:, :D_MODEL])
    weaver.after_matmul(2 * k)
    conv_finish()
    u = u_lin * jax.nn.sigmoid(_dot(h_ref[...], wc_ref[:, D_MODEL:]))

    tile = jnp.minimum(step, pl.num_programs(0) - 2)
    tail = tmaj_ref[pl.ds(tm * SUBLANES, halo_rows), :]
    tmaj_ref[pl.ds(0, halo_rows), :] = jnp.where(tile % tiles_per_seq == 0, jnp.zeros_like(tail), tail)
    for c in range(SUBLANES):
        tmaj_ref[pl.ds(halo_rows + c, tm, stride=SUBLANES), :] = u[:, c * LANES:(c + 1) * LANES]


def _front(x, n1, wg, wu, wd, nm, wq, wkv, wc, wgt, ones_q, ones_k, gq, gk, c_tab, s_fwd, s_bwd,
           dw_w, dw_b, ln_g, ln_b, tm, seq):
    t = x.shape[0]
    n_tiles = t // tm
    tiles_per_seq = seq // tm
    cur = lambda s: jnp.minimum(s, n_tiles - 1)
    row = lambda w: pl.BlockSpec((tm, w), lambda s: (cur(s), 0))
    lag = pl.BlockSpec((tm, D_MODEL), lambda s: (jnp.maximum(s - 1, 0), 0))
    tab = pl.BlockSpec((tm, LANES), lambda s: (cur(s) % tiles_per_seq, 0))
    vec = _const_spec((1, D_MODEL))
    out = lambda w, dt=BF16: jax.ShapeDtypeStruct((t, w), dt)
    dw_tmaj = dw_w.reshape(CONV_WIDTH * SUBLANES, LANES)
    db_tmaj = dw_b.reshape(SUBLANES, LANES)
    return pl.pallas_call(
        functools.partial(_front_kernel, tm=tm, tiles_per_seq=tiles_per_seq),
        grid=(n_tiles + 1,),
        in_specs=[row(D_MODEL), vec, _const_spec(wg.shape), _const_spec(wu.shape), _const_spec(wd.shape), vec,
                  _const_spec(wq.shape), _const_spec(wkv.shape), _const_spec(wc.shape), _const_spec(wgt.shape),
                  _const_spec(ones_q.shape), _const_spec(ones_k.shape),
                  _const_spec((1, ATTN_W)), _const_spec((1, KV_W)), tab, tab, tab,
                  _const_spec(dw_tmaj.shape), _const_spec(db_tmaj.shape), vec, vec],
        out_specs=[row(D_MODEL), row(ATTN_W), row(KV_W), row(KV_W), row(D_MODEL), row(D_MODEL), lag],
        out_shape=[out(D_MODEL, F32), out(ATTN_W), out(KV_W), out(KV_W), out(D_MODEL), out(D_MODEL),
                   out(D_MODEL)],
        scratch_shapes=[pltpu.VMEM(((CONV_HALO + tm) * SUBLANES, LANES), F32),
                        pltpu.VMEM((tm * SUBLANES, LANES), F32),
                        pltpu.VMEM((tm, D_MODEL), BF16)],
        compiler_params=_params(1),
        name="ffn1_proj_conv",
    )(x, n1, wg, wu, wd, nm, wq, wkv, wc, wgt, ones_q, ones_k, gq, gk, c_tab, s_fwd, s_bwd,
      dw_tmaj, db_tmaj, ln_g, ln_b)


def _attn_kernel(sinks_ref, q_ref, k_ref, v_ref, o_ref, *, q_tile):
    tile_start = pl.program_id(1) * q_tile
    lane = lax.broadcasted_iota(jnp.int32, (2 * BLOCK, LANES), 1)
    lo = lane < HEAD_DIM
    tiles_per_kv = ATTN_W // LANES // N_KV_HEADS
    qi = lax.broadcasted_iota(jnp.int32, (tiles_per_kv * BLOCK, 2 * BLOCK), 0) % BLOCK
    kj = lax.broadcasted_iota(jnp.int32, (tiles_per_kv * BLOCK, 2 * BLOCK), 1)
    delta = qi + BLOCK - kj
    band = (delta >= 0) & (delta < WINDOW)
    out_lo = lax.broadcasted_iota(jnp.int32, (tiles_per_kv * BLOCK, LANES), 1) < HEAD_DIM
    key_lo = lax.broadcasted_iota(jnp.int32, (2 * 2 * BLOCK, LANES), 0) < 2 * BLOCK
    ones_hat = (key_lo == out_lo).astype(BF16)

    for blk in range(q_tile // BLOCK):
        start = pl.multiple_of(tile_start + blk * BLOCK, BLOCK)
        prev = pl.multiple_of(jnp.maximum(start - BLOCK, 0), BLOCK)
        valid = band & (start + kj - BLOCK >= 0)
        rows = pl.ds(blk * BLOCK, BLOCK)

        def band_of(ref):
            return jnp.concatenate([ref[pl.ds(prev, BLOCK), :], ref[pl.ds(start, BLOCK), :]], axis=0)

        kf, vf = band_of(k_ref), band_of(v_ref)
        kr, vr = pltpu.roll(kf, HEAD_DIM, axis=1), pltpu.roll(vf, HEAD_DIM, axis=1)
        zero = jnp.zeros_like(kf)
        for h in range(N_KV_HEADS):
            k_a, k_b = (kf, kr) if h == 0 else (kr, kf)
            v_a, v_b = (vf, vr) if h == 0 else (vr, vf)
            k_hat = jnp.concatenate([jnp.where(lo, k_a, zero), jnp.where(lo, zero, k_b)], axis=0)
            v_hat = jnp.concatenate([jnp.where(lo, v_a, zero), jnp.where(lo, zero, v_b)], axis=0)
            q_h = jnp.concatenate(
                [q_ref[rows, pl.ds((h * tiles_per_kv + c) * LANES, LANES)] for c in range(tiles_per_kv)], axis=0)
            s = lax.dot_general(q_h, k_hat, (((1,), (1,)), ((), ())), preferred_element_type=F32)
            probs, sink_terms = [], []
            for half in range(2):
                sh = jnp.where(valid, s[:, half * 2 * BLOCK:(half + 1) * 2 * BLOCK], NEG_INF)
                sink = jnp.concatenate(
                    [jnp.full((BLOCK, LANES), sinks_ref[2 * (h * tiles_per_kv + c) + half] * LOG2E, F32)
                     for c in range(tiles_per_kv)], axis=0)
                m = jnp.max(jnp.maximum(jnp.maximum(sh[:, :LANES], sh[:, LANES:]), sink), axis=-1, keepdims=True)
                probs.append(jnp.exp2(sh - m).astype(BF16))
                sink_terms.append(jnp.exp2(sink - m))
            o_aug = _dot(jnp.concatenate(probs, axis=1), jnp.concatenate([v_hat, ones_hat], axis=1))
            denom = o_aug[:, LANES:] + jnp.where(out_lo, sink_terms[0], sink_terms[1])
            o2 = (o_aug[:, :LANES] / denom).astype(BF16)
            for c in range(tiles_per_kv):
                o_ref[rows, pl.ds((h * tiles_per_kv + c) * LANES, LANES)] = o2[c * BLOCK:(c + 1) * BLOCK]


def _attention(q, k, v, sinks, batch, seq, q_tile):
    t = q.shape[0]
    tiles = seq // q_tile
    return pl.pallas_call(
        functools.partial(_attn_kernel, q_tile=q_tile),
        grid_spec=pltpu.PrefetchScalarGridSpec(
            num_scalar_prefetch=1,
            grid=(batch, tiles),
            in_specs=[pl.BlockSpec((q_tile, ATTN_W), lambda b, i, s: (b * tiles + i, 0)),
                      pl.BlockSpec((seq, KV_W), lambda b, i, s: (b, 0)),
                      pl.BlockSpec((seq, KV_W), lambda b, i, s: (b, 0))],
            out_specs=pl.BlockSpec((q_tile, ATTN_W), lambda b, i, s: (b * tiles + i, 0)),
        ),
        out_shape=jax.ShapeDtypeStruct((t, ATTN_W), BF16),
        compiler_params=_params(2),
        name="swa",
    )(sinks, q, k, v)


def _merge_kernel(x_ref, a_ref, c_ref, ga_ref, gb_ref, wa_ref, wc_ref, wo_ref, o_ref):
    c = _dot(c_ref[...], wc_ref[...])
    a = _dot(a_ref[...], wa_ref[...])
    merged = (ga_ref[...].astype(F32) * a + gb_ref[...].astype(F32) * c).astype(BF16)
    o_ref[...] = x_ref[...] + _dot(merged, wo_ref[...])


def _merge(x, a, c_act, ga, gb, wa, wc, wo, tm):
    t = x.shape[0]
    row = pl.BlockSpec((tm, D_MODEL), lambda i: (i, 0))
    mat = _const_spec((D_MODEL, D_MODEL))
    return pl.pallas_call(
        _merge_kernel,
        grid=(t // tm,),
        in_specs=[row, row, row, row, row, mat, mat, mat],
        out_specs=row,
        out_shape=jax.ShapeDtypeStruct((t, D_MODEL), F32),
        compiler_params=_params(1),
        name="merge",
    )(x, a, c_act, ga, gb, wa, wc, wo)


def _rope_tables(seq):
    half = ROPE_DIM // 2
    inv_freq = ROPE_THETA ** (-jnp.arange(0, ROPE_DIM, 2, dtype=F32) / ROPE_DIM)
    per_row = LANES // half
    lane = np.arange(LANES)
    pos = (jnp.arange(seq // per_row, dtype=jnp.int32)[:, None] * per_row + (lane // half)[None, :]).astype(F32)
    ang = pos * inv_freq[lane % half][None, :]
    cos, sin = jnp.cos(ang).reshape(seq, half), jnp.sin(ang).reshape(seq, half)
    d = np.arange(LANES) % HEAD_DIM
    cos_l, sin_l = jnp.ones((seq, LANES), F32), jnp.zeros((seq, LANES), F32)
    for j in range(half):
        pick = (d < ROPE_DIM) & (d % half == j)
        cos_l = jnp.where(pick, cos[:, j:j + 1], cos_l)
        sin_l = jnp.where(pick, sin[:, j:j + 1], sin_l)
    return cos_l, jnp.where(d < half, -sin_l, 0.0), jnp.where(d >= half, sin_l, 0.0)


def _block_ones(width):
    head = np.arange(width) // HEAD_DIM
    return jnp.asarray(head[:, None] == head[None, :], BF16)


def kernel(x, p, ffn1_norm, ffn1_wgu, ffn1_wdown, mix_norm, w_in, q_norm, k_norm, sinks, attn_proj, dw_w, dw_b, conv_ln_g, conv_ln_b, conv_proj, w_out, ffn2_norm, ffn2_wgu, ffn2_wdown, ple_proj, ple_norm, ple_gate_norm, ple_gate_w):
    batch, seq, _ = x.shape
    depth = ffn1_norm.shape[0]
    t = batch * seq
    xt = x.reshape(t, D_MODEL)
    c_tab, s_fwd, s_bwd = _rope_tables(seq)
    ones_q, ones_k = _block_ones(2 * LANES), _block_ones(KV_W)
    vec = lambda a: a.reshape(1, -1)
    bf = lambda a: a.astype(BF16)
    o_k = ATTN_W
    o_c = o_k + 2 * KV_W
    o_g = o_c + 2 * D_MODEL
    for i in range(depth):
        wi = w_in[i]
        xt, q, k, v, ga, gb, c_act = _front(
            xt, vec(ffn1_norm[i]), bf(ffn1_wgu[i][:, :D_FF]), bf(ffn1_wgu[i][:, D_FF:]), bf(ffn1_wdown[i]),
            vec(mix_norm[i]), bf(wi[:, :o_k]), bf(wi[:, o_k:o_c]), bf(wi[:, o_c:o_g]), bf(wi[:, o_g:]),
            ones_q, ones_k, vec(jnp.tile(q_norm[i], N_Q_HEADS)), vec(jnp.tile(k_norm[i], N_KV_HEADS)),
            c_tab, s_fwd, s_bwd, dw_w[i], dw_b[i], vec(conv_ln_g[i]), vec(conv_ln_b[i]), FRONT_TILE, seq)
        a = _attention(q, k, v, sinks[i], batch, seq, TILE)
        xt = _merge(xt, a, c_act, ga, gb, bf(attn_proj[i]), bf(conv_proj[i]), bf(w_out[i]), TILE)
        xt = _ffn_ple(xt, p[i].reshape(t, PLE_DIM), vec(ffn2_norm[i]), bf(ffn2_wgu[i][:, :D_FF]),
                      bf(ffn2_wgu[i][:, D_FF:]), bf(ffn2_wdown[i]), bf(ple_proj[i]), vec(ple_norm[i]),
                      vec(ple_gate_norm[i]), bf(ple_gate_w[i]), TILE)
    return xt.reshape(batch, seq, D_MODEL)
```

```python
import functools

import jax
import jax.numpy as jnp
import numpy as np
from jax import lax
from jax.experimental import pallas as pl
from jax.experimental.pallas import tpu as pltpu

D_MODEL = 1024
N_Q_HEADS = 16
N_KV_HEADS = 2
HEAD_DIM = 64
WINDOW = 128
BLOCK = 128
ROPE_THETA = 500000.0
ROPE_DIM = HEAD_DIM // 4
ATTN_W = N_Q_HEADS * HEAD_DIM
KV_W = N_KV_HEADS * HEAD_DIM
CONV_WIDTH = 31
D_FF = 2816
PLE_DIM = 256
EPS = 1e-6
NEG_INF = -1e30
IN_K = ATTN_W
IN_C = IN_K + 2 * KV_W
IN_G = IN_C + 2 * D_MODEL
LOG2E = 1.4426950408889634

LANES = 128
SUBLANES = 8
assert D_MODEL == SUBLANES * LANES
CONV_HALO = 32
CONV_STEPS = 8
LN_ROWS = 32
FF_CHUNK = 256
assert D_FF % FF_CHUNK == 0
VMEM_LIMIT = 60 * 1024 * 1024
FRONT_TILE = 512
TILE = 512

BF16 = jnp.bfloat16
F32 = jnp.float32


def _dot(a, b):
    return jnp.dot(a, b, preferred_element_type=F32)


def _rmsnorm(x, g):
    return x * lax.rsqrt(jnp.mean(x * x, axis=-1, keepdims=True) + EPS) * g


def _const_spec(shape):
    return pl.BlockSpec(shape, lambda *_: (0,) * len(shape), pipeline_mode=pl.Buffered(1))


def _params(n_axes, flags=None):
    return pltpu.CompilerParams(dimension_semantics=("arbitrary",) * n_axes,
                                vmem_limit_bytes=VMEM_LIMIT, flags=flags)


def _swiglu_update(x, g_ref, wgu_ref, wd_ref):
    h = _rmsnorm(x, g_ref[...]).astype(BF16)
    g = _dot(h, wgu_ref[:, :D_FF])
    u = _dot(h, wgu_ref[:, D_FF:])
    a = (g * jax.nn.sigmoid(g) * u).astype(BF16)
    return x + 0.5 * _dot(a, wd_ref[...])


def _zero_after(values):
    tiles = [v[r:r + SUBLANES, c:c + LANES] for v in values
             for r in range(0, v.shape[0], SUBLANES) for c in range(0, v.shape[1], LANES)]
    half_word = jnp.uint32(16)
    bits = functools.reduce(jnp.bitwise_or, [pltpu.bitcast(t, jnp.uint32) for t in tiles])
    return lax.shift_right_logical(lax.shift_right_logical(bits, half_word), half_word).astype(F32)


class _Weaver:
    def __init__(self, lhs_ref, piece):
        self.lhs_ref, self.piece, self.zero = lhs_ref, piece, None

    def after_matmul(self, n=1):
        tile = (pl.ds(0, 2 * SUBLANES), pl.ds(0, LANES))
        if self.zero is not None:
            z = jnp.concatenate([self.zero, self.zero], axis=0)
            self.lhs_ref[tile] = (self.lhs_ref[tile].astype(F32) + z).astype(BF16)
        results = []
        for _ in range(n):
            results.extend(self.piece() or [])
        self.zero = _zero_after(results) if results else None


def _swiglu_update_chunked(x, g_ref, wgu_ref, wd_ref, weaver, n_pieces):
    h_ref = weaver.lhs_ref
    h_ref[...] = _rmsnorm(x, g_ref[...]).astype(BF16)
    y = None
    for c in range(0, D_FF, FF_CHUNK):
        g = _dot(h_ref[...], wgu_ref[:, c:c + FF_CHUNK])
        weaver.after_matmul(n_pieces)
        u = _dot(h_ref[...], wgu_ref[:, D_FF + c:D_FF + c + FF_CHUNK])
        weaver.after_matmul(n_pieces)
        a = (g * jax.nn.sigmoid(g) * u).astype(BF16)
        d = _dot(a, wd_ref[c:c + FF_CHUNK, :])
        weaver.after_matmul(n_pieces)
        y = d if y is None else y + d
    return x + 0.5 * y


def _ffn_ple_kernel(x_ref, p_ref, g_ref, wgu_ref, wd_ref, pw_ref, pn_ref, gn_ref, gw_ref, o_ref):
    x = _swiglu_update(x_ref[...], g_ref, wgu_ref, wd_ref)
    e = _rmsnorm(_dot(p_ref[...].astype(BF16), pw_ref[...]), pn_ref[...])
    gate = jax.nn.sigmoid(_dot(_rmsnorm(x, gn_ref[...]).astype(BF16), gw_ref[...]))
    o_ref[...] = x + gate * e


def _ffn_ple(x, p, norm, wgu, wd, ple_w, ple_norm, gate_norm, gate_w, tm):
    t = x.shape[0]
    row = pl.BlockSpec((tm, D_MODEL), lambda i: (i, 0))
    vec = _const_spec((1, D_MODEL))
    return pl.pallas_call(
        _ffn_ple_kernel,
        grid=(t // tm,),
        in_specs=[row, pl.BlockSpec((tm, PLE_DIM), lambda i: (i, 0)), vec,
                  _const_spec((D_MODEL, 2 * D_FF)), _const_spec((D_FF, D_MODEL)),
                  _const_spec((PLE_DIM, D_MODEL)), vec, vec, _const_spec((D_MODEL, D_MODEL))],
        out_specs=row,
        out_shape=jax.ShapeDtypeStruct((t, D_MODEL), F32),
        compiler_params=_params(1),
        name="ffn2_ple",
    )(x, p, norm, wgu, wd, ple_w, ple_norm, gate_norm, gate_w)


def _head_norm_rope(z, ones_blk, gain, c_tab, s_fwd, s_bwd):
    w = z.shape[1]
    blk = ones_blk.shape[0]
    sq = (z * z).astype(BF16)
    ss = jnp.concatenate([_dot(sq[:, i:i + blk], ones_blk) for i in range(0, w, blk)], axis=1)
    t = z * lax.rsqrt(ss * (1.0 / HEAD_DIM) + EPS) * gain
    half = ROPE_DIM // 2
    up = pltpu.roll(t, w - half, axis=1)
    dn = pltpu.roll(t, half, axis=1)
    reps = w // LANES
    tile = lambda a: jnp.concatenate([a] * reps, axis=1) if reps > 1 else a
    return t * tile(c_tab) + up * tile(s_fwd) + dn * tile(s_bwd)


def _conv_ln_silu(tmaj_ref, y_ref, dw_ref, db_ref, lg_ref, lb_ref, out_ref, tm):
    base = CONV_HALO - (CONV_WIDTH - 1)

    def conv_piece(t0):
        acc = [None] * CONV_STEPS
        for j in range(CONV_WIDTH):
            w_j = dw_ref[pl.ds(j * SUBLANES, SUBLANES), :]
            for r in range(CONV_STEPS):
                v = tmaj_ref[pl.ds((t0 + r + base + j) * SUBLANES, SUBLANES), :] * w_j
                acc[r] = v if j == 0 else acc[r] + v
        bias = db_ref[...]
        acc = [a + bias for a in acc]
        for r in range(CONV_STEPS):
            y_ref[pl.ds((t0 + r) * SUBLANES, SUBLANES), :] = acc[r]
        return acc

    def ln_piece(t0):
        y = jnp.concatenate([y_ref[pl.ds(t0 * SUBLANES + c, LN_ROWS, stride=SUBLANES), :]
                             for c in range(SUBLANES)], axis=1)
        yc = y - jnp.mean(y, axis=-1, keepdims=True)
        var = jnp.mean(yc * yc, axis=-1, keepdims=True)
        ln = yc * lax.rsqrt(var + EPS) * lg_ref[...] + lb_ref[...]
        act = ln * jax.nn.sigmoid(ln)
        out_ref[pl.ds(t0, LN_ROWS), :] = act.astype(BF16)
        return [act]

    pending = [functools.partial(conv_piece, t0) for t0 in range(0, tm, CONV_STEPS)]
    pending += [functools.partial(ln_piece, t0) for t0 in range(0, tm, LN_ROWS)]

    def piece():
        return pending.pop(0)() if pending else None

    def finish():
        while pending:
            piece()

    return piece, finish


def _front_kernel(x_ref, n1_ref, wgu_ref, wd_ref, nm_ref, win_ref,
                  oq_ref, ok_ref, gq_ref, gk_ref, c_ref, sf_ref, sb_ref, dw_ref, db_ref, lg_ref, lb_ref,
                  x1_out, q_out, k_out, v_out, ga_out, gb_out, cact_out, tmaj_ref, y_ref, h_ref, *, tm, tiles_per_seq):
    step = pl.program_id(0)
    halo_rows = CONV_HALO * SUBLANES

    @pl.when(step == 0)
    def _():
        tmaj_ref[...] = jnp.zeros(tmaj_ref.shape, F32)

    conv = functools.partial(_conv_ln_silu, tmaj_ref, y_ref, dw_ref, db_ref, lg_ref, lb_ref, cact_out, tm)
    last = pl.num_programs(0) - 1

    @pl.when(step == last)
    def _():
        conv()[1]()

    @pl.when(step < last)
    def _():
        conv_piece, conv_finish = conv()
        weaver = _Weaver(h_ref, conv_piece)
        ffn_matmuls = 3 * (D_FF // FF_CHUNK)
        x1 = _swiglu_update_chunked(x_ref[...], n1_ref, wgu_ref, wd_ref, weaver,
                                    pl.cdiv(tm // CONV_STEPS, ffn_matmuls))
        k = tm // (8 * LN_ROWS)
        x1_out[...] = x1
        h_ref[...] = _rmsnorm(x1, nm_ref[...]).astype(BF16)
        c_tab, s_fwd, s_bwd = c_ref[...], sf_ref[...], sb_ref[...]
        zq = _dot(h_ref[...], win_ref[:, :IN_K])
        weaver.after_matmul(k)
        scale = HEAD_DIM ** -0.5 * LOG2E
        q_out[...] = (_head_norm_rope(zq, oq_ref[...], gq_ref[...], c_tab, s_fwd, s_bwd) * scale).astype(BF16)
        zkv = _dot(h_ref[...], win_ref[:, IN_K:IN_C])
        weaver.after_matmul(k)
        k_out[...] = _head_norm_rope(zkv[:, :KV_W], ok_ref[...], gk_ref[...], c_tab, s_fwd, s_bwd).astype(BF16)
        v_out[...] = zkv[:, KV_W:].astype(BF16)
        za = _dot(h_ref[...], win_ref[:, IN_G:IN_G + D_MODEL])
        weaver.after_matmul(2 * k)
        ga_out[...] = jax.nn.sigmoid(za).astype(BF16)
        zb = _dot(h_ref[...], win_ref[:, IN_G + D_MODEL:])
        weaver.after_matmul(2 * k)
        gb_out[...] = jax.nn.sigmoid(zb).astype(BF16)
        u_lin = _dot(h_ref[...], win_ref[:, IN_C:IN_C + D_MODEL])
        weaver.after_matmul(2 * k)
        conv_finish()
        u = u_lin * jax.nn.sigmoid(_dot(h_ref[...], win_ref[:, IN_C + D_MODEL:IN_G]))

        tail = tmaj_ref[pl.ds(tm * SUBLANES, halo_rows), :]
        tmaj_ref[pl.ds(0, halo_rows), :] = jnp.where(step % tiles_per_seq == 0, jnp.zeros_like(tail), tail)
        for c in range(SUBLANES):
            tmaj_ref[pl.ds(halo_rows + c, tm, stride=SUBLANES), :] = u[:, c * LANES:(c + 1) * LANES]


def _front(x, n1, wgu, wd, nm, w_in, ones_q, ones_k, gq, gk, c_tab, s_fwd, s_bwd,
           dw_w, dw_b, ln_g, ln_b, tm, seq):
    t = x.shape[0]
    n_tiles = t // tm
    tiles_per_seq = seq // tm
    cur = lambda s: jnp.minimum(s, n_tiles - 1)
    row = lambda w: pl.BlockSpec((tm, w), lambda s: (cur(s), 0))
    lag = pl.BlockSpec((tm, D_MODEL), lambda s: (jnp.maximum(s - 1, 0), 0))
    tab = pl.BlockSpec((tm, LANES), lambda s: (cur(s) % tiles_per_seq, 0))
    vec = _const_spec((1, D_MODEL))
    out = lambda w, dt=BF16: jax.ShapeDtypeStruct((t, w), dt)
    dw_tmaj = dw_w.reshape(CONV_WIDTH * SUBLANES, LANES)
    db_tmaj = dw_b.reshape(SUBLANES, LANES)
    return pl.pallas_call(
        functools.partial(_front_kernel, tm=tm, tiles_per_seq=tiles_per_seq),
        grid=(n_tiles + 1,),
        in_specs=[row(D_MODEL), vec, _const_spec(wgu.shape), _const_spec(wd.shape), vec, _const_spec(w_in.shape),
                  _const_spec(ones_q.shape), _const_spec(ones_k.shape),
                  _const_spec((1, ATTN_W)), _const_spec((1, KV_W)), tab, tab, tab,
                  _const_spec(dw_tmaj.shape), _const_spec(db_tmaj.shape), vec, vec],
        out_specs=[row(D_MODEL), row(ATTN_W), row(KV_W), row(KV_W), row(D_MODEL), row(D_MODEL), lag],
        out_shape=[out(D_MODEL, F32), out(ATTN_W), out(KV_W), out(KV_W), out(D_MODEL), out(D_MODEL),
                   out(D_MODEL)],
        scratch_shapes=[pltpu.VMEM(((CONV_HALO + tm) * SUBLANES, LANES), F32),
                        pltpu.VMEM((tm * SUBLANES, LANES), F32),
                        pltpu.VMEM((tm, D_MODEL), BF16)],
        compiler_params=_params(1),
        name="ffn1_proj_conv",
    )(x, n1, wgu, wd, nm, w_in, ones_q, ones_k, gq, gk, c_tab, s_fwd, s_bwd,
      dw_tmaj, db_tmaj, ln_g, ln_b)


def _attn_kernel(sinks_ref, q_ref, k_ref, v_ref, o_ref, *, q_tile):
    tile_start = pl.program_id(1) * q_tile
    lane = lax.broadcasted_iota(jnp.int32, (2 * BLOCK, LANES), 1)
    lo = lane < HEAD_DIM
    tiles_per_kv = ATTN_W // LANES // N_KV_HEADS
    qi = lax.broadcasted_iota(jnp.int32, (tiles_per_kv * BLOCK, 2 * BLOCK), 0) % BLOCK
    kj = lax.broadcasted_iota(jnp.int32, (tiles_per_kv * BLOCK, 2 * BLOCK), 1)
    delta = qi + BLOCK - kj
    band = (delta >= 0) & (delta < WINDOW)
    out_lo = lax.broadcasted_iota(jnp.int32, (tiles_per_kv * BLOCK, LANES), 1) < HEAD_DIM
    key_lo = lax.broadcasted_iota(jnp.int32, (2 * 2 * BLOCK, LANES), 0) < 2 * BLOCK
    ones_hat = (key_lo == out_lo).astype(BF16)

    for blk in range(q_tile // BLOCK):
        start = pl.multiple_of(tile_start + blk * BLOCK, BLOCK)
        prev = pl.multiple_of(jnp.maximum(start - BLOCK, 0), BLOCK)
        valid = band & (start + kj - BLOCK >= 0)
        rows = pl.ds(blk * BLOCK, BLOCK)

        def band_of(ref):
            return jnp.concatenate([ref[pl.ds(prev, BLOCK), :], ref[pl.ds(start, BLOCK), :]], axis=0)

        kf, vf = band_of(k_ref), band_of(v_ref)
        kr, vr = pltpu.roll(kf, HEAD_DIM, axis=1), pltpu.roll(vf, HEAD_DIM, axis=1)
        zero = jnp.zeros_like(kf)
        for h in range(N_KV_HEADS):
            k_a, k_b = (kf, kr) if h == 0 else (kr, kf)
            v_a, v_b = (vf, vr) if h == 0 else (vr, vf)
            k_hat = jnp.concatenate([jnp.where(lo, k_a, zero), jnp.where(lo, zero, k_b)], axis=0)
            v_hat = jnp.concatenate([jnp.where(lo, v_a, zero), jnp.where(lo, zero, v_b)], axis=0)
            q_h = jnp.concatenate(
                [q_ref[rows, pl.ds((h * tiles_per_kv + c) * LANES, LANES)] for c in range(tiles_per_kv)], axis=0)
            s = lax.dot_general(q_h, k_hat, (((1,), (1,)), ((), ())), preferred_element_type=F32)
            probs, sink_terms = [], []
            for half in range(2):
                sh = jnp.where(valid, s[:, half * 2 * BLOCK:(half + 1) * 2 * BLOCK], NEG_INF)
                sink = jnp.concatenate(
                    [jnp.full((BLOCK, LANES), sinks_ref[2 * (h * tiles_per_kv + c) + half] * LOG2E, F32)
                     for c in range(tiles_per_kv)], axis=0)
                m = jnp.max(jnp.maximum(jnp.maximum(sh[:, :LANES], sh[:, LANES:]), sink), axis=-1, keepdims=True)
                probs.append(jnp.exp2(sh - m).astype(BF16))
                sink_terms.append(jnp.exp2(sink - m))
            o_aug = _dot(jnp.concatenate(probs, axis=1), jnp.concatenate([v_hat, ones_hat], axis=1))
            denom = o_aug[:, LANES:] + jnp.where(out_lo, sink_terms[0], sink_terms[1])
            o2 = (o_aug[:, :LANES] / denom).astype(BF16)
            for c in range(tiles_per_kv):
                o_ref[rows, pl.ds((h * tiles_per_kv + c) * LANES, LANES)] = o2[c * BLOCK:(c + 1) * BLOCK]


def _attention(q, k, v, sinks, batch, seq, q_tile):
    t = q.shape[0]
    tiles = seq // q_tile
    return pl.pallas_call(
        functools.partial(_attn_kernel, q_tile=q_tile),
        grid_spec=pltpu.PrefetchScalarGridSpec(
            num_scalar_prefetch=1,
            grid=(batch, tiles),
            in_specs=[pl.BlockSpec((q_tile, ATTN_W), lambda b, i, s: (b * tiles + i, 0)),
                      pl.BlockSpec((seq, KV_W), lambda b, i, s: (b, 0)),
                      pl.BlockSpec((seq, KV_W), lambda b, i, s: (b, 0))],
            out_specs=pl.BlockSpec((q_tile, ATTN_W), lambda b, i, s: (b * tiles + i, 0)),
        ),
        out_shape=jax.ShapeDtypeStruct((t, ATTN_W), BF16),
        compiler_params=_params(2),
        name="swa",
    )(sinks, q, k, v)


def _merge_kernel(x_ref, a_ref, c_ref, ga_ref, gb_ref, wa_ref, wc_ref, wo_ref, o_ref):
    c = _dot(c_ref[...], wc_ref[...])
    a = _dot(a_ref[...], wa_ref[...])
    merged = (ga_ref[...].astype(F32) * a + gb_ref[...].astype(F32) * c).astype(BF16)
    o_ref[...] = x_ref[...] + _dot(merged, wo_ref[...])


def _merge(x, a, c_act, ga, gb, wa, wc, wo, tm):
    t = x.shape[0]
    row = pl.BlockSpec((tm, D_MODEL), lambda i: (i, 0))
    mat = _const_spec((D_MODEL, D_MODEL))
    return pl.pallas_call(
        _merge_kernel,
        grid=(t // tm,),
        in_specs=[row, row, row, row, row, mat, mat, mat],
        out_specs=row,
        out_shape=jax.ShapeDtypeStruct((t, D_MODEL), F32),
        compiler_params=_params(1),
        name="merge",
    )(x, a, c_act, ga, gb, wa, wc, wo)


def _rope_tables(seq):
    half = ROPE_DIM // 2
    inv_freq = ROPE_THETA ** (-jnp.arange(0, ROPE_DIM, 2, dtype=F32) / ROPE_DIM)
    ang = jnp.arange(seq, dtype=F32)[:, None] * inv_freq[None, :]
    cos, sin = jnp.cos(ang), jnp.sin(ang)
    d = np.arange(LANES) % HEAD_DIM
    cos_l, sin_l = jnp.ones((seq, LANES), F32), jnp.zeros((seq, LANES), F32)
    for j in range(half):
        pick = (d < ROPE_DIM) & (d % half == j)
        cos_l = jnp.where(pick, cos[:, j:j + 1], cos_l)
        sin_l = jnp.where(pick, sin[:, j:j + 1], sin_l)
    return cos_l, jnp.where(d < half, -sin_l, 0.0), jnp.where(d >= half, sin_l, 0.0)


def _block_ones(width):
    head = np.arange(width) // HEAD_DIM
    return jnp.asarray(head[:, None] == head[None, :], BF16)


def kernel(x, p, ffn1_norm, ffn1_wgu, ffn1_wdown, mix_norm, w_in, q_norm, k_norm, sinks, attn_proj, dw_w, dw_b, conv_ln_g, conv_ln_b, conv_proj, w_out, ffn2_norm, ffn2_wgu, ffn2_wdown, ple_proj, ple_norm, ple_gate_norm, ple_gate_w):
    batch, seq, _ = x.shape
    depth = ffn1_norm.shape[0]
    t = batch * seq
    xt = x.reshape(t, D_MODEL)
    c_tab, s_fwd, s_bwd = _rope_tables(seq)
    ones_q, ones_k = _block_ones(2 * LANES), _block_ones(KV_W)
    vec = lambda a: a.reshape(1, -1)
    bf = lambda a: a.astype(BF16)
    for i in range(depth):
        xt, q, k, v, ga, gb, c_act = _front(
            xt, vec(ffn1_norm[i]), bf(ffn1_wgu[i]), bf(ffn1_wdown[i]), vec(mix_norm[i]), bf(w_in[i]),
            ones_q, ones_k, vec(jnp.tile(q_norm[i], N_Q_HEADS)), vec(jnp.tile(k_norm[i], N_KV_HEADS)),
            c_tab, s_fwd, s_bwd, dw_w[i], dw_b[i], vec(conv_ln_g[i]), vec(conv_ln_b[i]), FRONT_TILE, seq)
        a = _attention(q, k, v, sinks[i], batch, seq, TILE)
        xt = _merge(xt, a, c_act, ga, gb, bf(attn_proj[i]), bf(conv_proj[i]), bf(w_out[i]), TILE)
        xt = _ffn_ple(xt, p[i].reshape(t, PLE_DIM), vec(ffn2_norm[i]), bf(ffn2_wgu[i]), bf(ffn2_wdown[i]),
                      bf(ple_proj[i]), vec(ple_norm[i]), vec(ple_gate_norm[i]), bf(ple_gate_w[i]), TILE)
    return xt.reshape(batch, seq, D_MODEL)
```

```python
import functools

import jax
import jax.numpy as jnp
import numpy as np
from jax import lax
from jax.experimental import pallas as pl
from jax.experimental.pallas import tpu as pltpu

D_MODEL = 1024
N_Q_HEADS = 16
N_KV_HEADS = 2
HEAD_DIM = 64
WINDOW = 128
BLOCK = 128
ROPE_THETA = 500000.0
ROPE_DIM = HEAD_DIM // 4
ATTN_W = N_Q_HEADS * HEAD_DIM
KV_W = N_KV_HEADS * HEAD_DIM
CONV_WIDTH = 31
D_FF = 2816
PLE_DIM = 256
EPS = 1e-6
NEG_INF = -1e30
IN_K = ATTN_W
IN_C = IN_K + 2 * KV_W
IN_G = IN_C + 2 * D_MODEL
LOG2E = 1.4426950408889634

LANES = 128
SUBLANES = 8
assert D_MODEL == SUBLANES * LANES
CONV_HALO = 32
CONV_STEPS = 8
LN_ROWS = 32
FF_CHUNK = 256
assert D_FF % FF_CHUNK == 0
VMEM_LIMIT = 60 * 1024 * 1024
FRONT_TILE = 512
TILE = 512

BF16 = jnp.bfloat16
F32 = jnp.float32


def _dot(a, b):
    return jnp.dot(a, b, preferred_element_type=F32)


def _rmsnorm(x, g):
    return x * lax.rsqrt(jnp.mean(x * x, axis=-1, keepdims=True) + EPS) * g


def _const_spec(shape):
    return pl.BlockSpec(shape, lambda *_: (0,) * len(shape), pipeline_mode=pl.Buffered(1))


def _params(n_axes, flags=None):
    return pltpu.CompilerParams(dimension_semantics=("arbitrary",) * n_axes,
                                vmem_limit_bytes=VMEM_LIMIT, flags=flags)


def _swiglu_update(x, g_ref, wgu_ref, wd_ref):
    h = _rmsnorm(x, g_ref[...]).astype(BF16)
    g = _dot(h, wgu_ref[:, :D_FF])
    u = _dot(h, wgu_ref[:, D_FF:])
    a = (g * jax.nn.sigmoid(g) * u).astype(BF16)
    return x + 0.5 * _dot(a, wd_ref[...])


def _zero_after(values):
    tiles = [v[r:r + SUBLANES, c:c + LANES] for v in values
             for r in range(0, v.shape[0], SUBLANES) for c in range(0, v.shape[1], LANES)]
    half_word = jnp.uint32(16)
    bits = functools.reduce(jnp.bitwise_or, [pltpu.bitcast(t, jnp.uint32) for t in tiles])
    return lax.shift_right_logical(lax.shift_right_logical(bits, half_word), half_word).astype(F32)


class _Weaver:
    def __init__(self, lhs_ref, piece):
        self.lhs_ref, self.piece, self.zero = lhs_ref, piece, None

    def after_matmul(self, n=1):
        tile = (pl.ds(0, 2 * SUBLANES), pl.ds(0, LANES))
        if self.zero is not None:
            z = jnp.concatenate([self.zero, self.zero], axis=0)
            self.lhs_ref[tile] = (self.lhs_ref[tile].astype(F32) + z).astype(BF16)
        results = []
        for _ in range(n):
            results.extend(self.piece() or [])
        self.zero = _zero_after(results) if results else None


def _swiglu_update_chunked(x, g_ref, wgu_ref, wd_ref, weaver, n_pieces):
    h_ref = weaver.lhs_ref
    h_ref[...] = _rmsnorm(x, g_ref[...]).astype(BF16)
    y = None
    for c in range(0, D_FF, FF_CHUNK):
        g = _dot(h_ref[...], wgu_ref[:, c:c + FF_CHUNK])
        weaver.after_matmul(n_pieces)
        u = _dot(h_ref[...], wgu_ref[:, D_FF + c:D_FF + c + FF_CHUNK])
        weaver.after_matmul(n_pieces)
        a = (g * jax.nn.sigmoid(g) * u).astype(BF16)
        d = _dot(a, wd_ref[c:c + FF_CHUNK, :])
        weaver.after_matmul(n_pieces)
        y = d if y is None else y + d
    return x + 0.5 * y


def _ffn_ple_kernel(x_ref, p_ref, g_ref, wgu_ref, wd_ref, pw_ref, pn_ref, gn_ref, gw_ref, o_ref):
    x = _swiglu_update(x_ref[...], g_ref, wgu_ref, wd_ref)
    e = _rmsnorm(_dot(p_ref[...].astype(BF16), pw_ref[...]), pn_ref[...])
    gate = jax.nn.sigmoid(_dot(_rmsnorm(x, gn_ref[...]).astype(BF16), gw_ref[...]))
    o_ref[...] = x + gate * e


def _ffn_ple(x, p, norm, wgu, wd, ple_w, ple_norm, gate_norm, gate_w, tm):
    t = x.shape[0]
    row = pl.BlockSpec((tm, D_MODEL), lambda i: (i, 0))
    vec = _const_spec((1, D_MODEL))
    return pl.pallas_call(
        _ffn_ple_kernel,
        grid=(t // tm,),
        in_specs=[row, pl.BlockSpec((tm, PLE_DIM), lambda i: (i, 0)), vec,
                  _const_spec((D_MODEL, 2 * D_FF)), _const_spec((D_FF, D_MODEL)),
                  _const_spec((PLE_DIM, D_MODEL)), vec, vec, _const_spec((D_MODEL, D_MODEL))],
        out_specs=row,
        out_shape=jax.ShapeDtypeStruct((t, D_MODEL), F32),
        compiler_params=_params(1),
        name="ffn2_ple",
    )(x, p, norm, wgu, wd, ple_w, ple_norm, gate_norm, gate_w)


def _head_norm_rope(z, ones_blk, gain, c_tab, s_fwd, s_bwd):
    w = z.shape[1]
    blk = ones_blk.shape[0]
    sq = (z * z).astype(BF16)
    ss = jnp.concatenate([_dot(sq[:, i:i + blk], ones_blk) for i in range(0, w, blk)], axis=1)
    t = z * lax.rsqrt(ss * (1.0 / HEAD_DIM) + EPS) * gain
    half = ROPE_DIM // 2
    up = pltpu.roll(t, w - half, axis=1)
    dn = pltpu.roll(t, half, axis=1)
    reps = w // LANES
    tile = lambda a: jnp.concatenate([a] * reps, axis=1) if reps > 1 else a
    return t * tile(c_tab) + up * tile(s_fwd) + dn * tile(s_bwd)


def _conv_ln_silu(tmaj_ref, y_ref, dw_ref, db_ref, lg_ref, lb_ref, out_ref, tm):
    base = CONV_HALO - (CONV_WIDTH - 1)

    def conv_piece(t0):
        acc = [None] * CONV_STEPS
        for j in range(CONV_WIDTH):
            w_j = dw_ref[pl.ds(j * SUBLANES, SUBLANES), :]
            for r in range(CONV_STEPS):
                v = tmaj_ref[pl.ds((t0 + r + base + j) * SUBLANES, SUBLANES), :] * w_j
                acc[r] = v if j == 0 else acc[r] + v
        bias = db_ref[...]
        acc = [a + bias for a in acc]
        for r in range(CONV_STEPS):
            y_ref[pl.ds((t0 + r) * SUBLANES, SUBLANES), :] = acc[r]
        return acc

    def ln_piece(t0):
        y = jnp.concatenate([y_ref[pl.ds(t0 * SUBLANES + c, LN_ROWS, stride=SUBLANES), :]
                             for c in range(SUBLANES)], axis=1)
        yc = y - jnp.mean(y, axis=-1, keepdims=True)
        var = jnp.mean(yc * yc, axis=-1, keepdims=True)
        ln = yc * lax.rsqrt(var + EPS) * lg_ref[...] + lb_ref[...]
        act = ln * jax.nn.sigmoid(ln)
        out_ref[pl.ds(t0, LN_ROWS), :] = act.astype(BF16)
        return [act]

    pending = [functools.partial(conv_piece, t0) for t0 in range(0, tm, CONV_STEPS)]
    pending += [functools.partial(ln_piece, t0) for t0 in range(0, tm, LN_ROWS)]

    def piece():
        return pending.pop(0)() if pending else None

    def finish():
        while pending:
            piece()

    return piece, finish


def _front_kernel(x_ref, n1_ref, wgu_ref, wd_ref, nm_ref, win_ref,
                  oq_ref, ok_ref, gq_ref, gk_ref, c_ref, sf_ref, sb_ref, dw_ref, db_ref, lg_ref, lb_ref,
                  x1_out, q_out, k_out, v_out, ga_out, gb_out, cact_out, tmaj_ref, y_ref, h_ref, *, tm, tiles_per_seq):
    step = pl.program_id(0)
    halo_rows = CONV_HALO * SUBLANES

    @pl.when(step == 0)
    def _():
        tmaj_ref[...] = jnp.zeros(tmaj_ref.shape, F32)

    conv = functools.partial(_conv_ln_silu, tmaj_ref, y_ref, dw_ref, db_ref, lg_ref, lb_ref, cact_out, tm)
    last = pl.num_programs(0) - 1

    @pl.when(step == last)
    def _():
        conv()[1]()

    @pl.when(step < last)
    def _():
        conv_piece, conv_finish = conv()
        weaver = _Weaver(h_ref, conv_piece)
        ffn_matmuls = 3 * (D_FF // FF_CHUNK)
        x1 = _swiglu_update_chunked(x_ref[...], n1_ref, wgu_ref, wd_ref, weaver,
                                    pl.cdiv(tm // CONV_STEPS, ffn_matmuls))
        k = tm // (8 * LN_ROWS)
        x1_out[...] = x1
        h_ref[...] = _rmsnorm(x1, nm_ref[...]).astype(BF16)
        c_tab, s_fwd, s_bwd = c_ref[...], sf_ref[...], sb_ref[...]
        zq = _dot(h_ref[...], win_ref[:, :IN_K])
        weaver.after_matmul(k)
        scale = HEAD_DIM ** -0.5 * LOG2E
        q_out[...] = (_head_norm_rope(zq, oq_ref[...], gq_ref[...], c_tab, s_fwd, s_bwd) * scale).astype(BF16)
        zkv = _dot(h_ref[...], win_ref[:, IN_K:IN_C])
        weaver.after_matmul(k)
        k_out[...] = _head_norm_rope(zkv[:, :KV_W], ok_ref[...], gk_ref[...], c_tab, s_fwd, s_bwd).astype(BF16)
        v_out[...] = zkv[:, KV_W:].astype(BF16)
        za = _dot(h_ref[...], win_ref[:, IN_G:IN_G + D_MODEL])
        weaver.after_matmul(2 * k)
        ga_out[...] = jax.nn.sigmoid(za).astype(BF16)
        zb = _dot(h_ref[...], win_ref[:, IN_G + D_MODEL:])
        weaver.after_matmul(2 * k)
        gb_out[...] = jax.nn.sigmoid(zb).astype(BF16)
        u_lin = _dot(h_ref[...], win_ref[:, IN_C:IN_C + D_MODEL])
        weaver.after_matmul(2 * k)
        conv_finish()
        u = u_lin * jax.nn.sigmoid(_dot(h_ref[...], win_ref[:, IN_C + D_MODEL:IN_G]))

        tail = tmaj_ref[pl.ds(tm * SUBLANES, halo_rows), :]
        tmaj_ref[pl.ds(0, halo_rows), :] = jnp.where(step % tiles_per_seq == 0, jnp.zeros_like(tail), tail)
        for c in range(SUBLANES):
            tmaj_ref[pl.ds(halo_rows + c, tm, stride=SUBLANES), :] = u[:, c * LANES:(c + 1) * LANES]


def _front(x, n1, wgu, wd, nm, w_in, ones_q, ones_k, gq, gk, c_tab, s_fwd, s_bwd,
           dw_w, dw_b, ln_g, ln_b, tm, seq):
    t = x.shape[0]
    n_tiles = t // tm
    tiles_per_seq = seq // tm
    cur = lambda s: jnp.minimum(s, n_tiles - 1)
    row = lambda w: pl.BlockSpec((tm, w), lambda s: (cur(s), 0))
    lag = pl.BlockSpec((tm, D_MODEL), lambda s: (jnp.maximum(s - 1, 0), 0))
    tab = pl.BlockSpec((tm, LANES), lambda s: (cur(s) % tiles_per_seq, 0))
    vec = _const_spec((1, D_MODEL))
    out = lambda w, dt=BF16: jax.ShapeDtypeStruct((t, w), dt)
    dw_tmaj = dw_w.reshape(CONV_WIDTH * SUBLANES, LANES)
    db_tmaj = dw_b.reshape(SUBLANES, LANES)
    return pl.pallas_call(
        functools.partial(_front_kernel, tm=tm, tiles_per_seq=tiles_per_seq),
        grid=(n_tiles + 1,),
        in_specs=[row(D_MODEL), vec, _const_spec(wgu.shape), _const_spec(wd.shape), vec, _const_spec(w_in.shape),
                  _const_spec(ones_q.shape), _const_spec(ones_k.shape),
                  _const_spec((1, ATTN_W)), _const_spec((1, KV_W)), tab, tab, tab,
                  _const_spec(dw_tmaj.shape), _const_spec(db_tmaj.shape), vec, vec],
        out_specs=[row(D_MODEL), row(ATTN_W), row(KV_W), row(KV_W), row(D_MODEL), row(D_MODEL), lag],
        out_shape=[out(D_MODEL, F32), out(ATTN_W), out(KV_W), out(KV_W), out(D_MODEL), out(D_MODEL),
                   out(D_MODEL)],
        scratch_shapes=[pltpu.VMEM(((CONV_HALO + tm) * SUBLANES, LANES), F32),
                        pltpu.VMEM((tm * SUBLANES, LANES), F32),
                        pltpu.VMEM((tm, D_MODEL), BF16)],
        compiler_params=_params(1),
        name="ffn1_proj_conv",
    )(x, n1, wgu, wd, nm, w_in, ones_q, ones_k, gq, gk, c_tab, s_fwd, s_bwd,
      dw_tmaj, db_tmaj, ln_g, ln_b)


def _attn_kernel(sinks_ref, q_ref, k_ref, v_ref, *rest, q_tile, n_casts):
    o_ref = rest[n_casts]
    for src, dst in zip(rest[:n_casts], rest[n_casts + 1:]):
        dst[...] = src[...].astype(BF16)

    tile_start = pl.program_id(1) * q_tile
    lane = lax.broadcasted_iota(jnp.int32, (2 * BLOCK, LANES), 1)
    lo = lane < HEAD_DIM
    tiles_per_kv = ATTN_W // LANES // N_KV_HEADS
    qi = lax.broadcasted_iota(jnp.int32, (tiles_per_kv * BLOCK, 2 * BLOCK), 0) % BLOCK
    kj = lax.broadcasted_iota(jnp.int32, (tiles_per_kv * BLOCK, 2 * BLOCK), 1)
    delta = qi + BLOCK - kj
    band = (delta >= 0) & (delta < WINDOW)
    out_lo = lax.broadcasted_iota(jnp.int32, (tiles_per_kv * BLOCK, LANES), 1) < HEAD_DIM
    key_lo = lax.broadcasted_iota(jnp.int32, (2 * 2 * BLOCK, LANES), 0) < 2 * BLOCK
    ones_hat = (key_lo == out_lo).astype(BF16)

    for blk in range(q_tile // BLOCK):
        start = pl.multiple_of(tile_start + blk * BLOCK, BLOCK)
        prev = pl.multiple_of(jnp.maximum(start - BLOCK, 0), BLOCK)
        valid = band & (start + kj - BLOCK >= 0)
        rows = pl.ds(blk * BLOCK, BLOCK)

        def band_of(ref):
            return jnp.concatenate([ref[pl.ds(prev, BLOCK), :], ref[pl.ds(start, BLOCK), :]], axis=0)

        kf, vf = band_of(k_ref), band_of(v_ref)
        kr, vr = pltpu.roll(kf, HEAD_DIM, axis=1), pltpu.roll(vf, HEAD_DIM, axis=1)
        zero = jnp.zeros_like(kf)
        for h in range(N_KV_HEADS):
            k_a, k_b = (kf, kr) if h == 0 else (kr, kf)
            v_a, v_b = (vf, vr) if h == 0 else (vr, vf)
            k_hat = jnp.concatenate([jnp.where(lo, k_a, zero), jnp.where(lo, zero, k_b)], axis=0)
            v_hat = jnp.concatenate([jnp.where(lo, v_a, zero), jnp.where(lo, zero, v_b)], axis=0)
            q_h = jnp.concatenate(
                [q_ref[rows, pl.ds((h * tiles_per_kv + c) * LANES, LANES)] for c in range(tiles_per_kv)], axis=0)
            s = lax.dot_general(q_h, k_hat, (((1,), (1,)), ((), ())), preferred_element_type=F32)
            probs, sink_terms = [], []
            for half in range(2):
                sh = jnp.where(valid, s[:, half * 2 * BLOCK:(half + 1) * 2 * BLOCK], NEG_INF)
                sink = jnp.concatenate(
                    [jnp.full((BLOCK, LANES), sinks_ref[2 * (h * tiles_per_kv + c) + half] * LOG2E, F32)
                     for c in range(tiles_per_kv)], axis=0)
                m = jnp.max(jnp.maximum(jnp.maximum(sh[:, :LANES], sh[:, LANES:]), sink), axis=-1, keepdims=True)
                probs.append(jnp.exp2(sh - m).astype(BF16))
                sink_terms.append(jnp.exp2(sink - m))
            o_aug = _dot(jnp.concatenate(probs, axis=1), jnp.concatenate([v_hat, ones_hat], axis=1))
            denom = o_aug[:, LANES:] + jnp.where(out_lo, sink_terms[0], sink_terms[1])
            o2 = (o_aug[:, :LANES] / denom).astype(BF16)
            for c in range(tiles_per_kv):
                o_ref[rows, pl.ds((h * tiles_per_kv + c) * LANES, LANES)] = o2[c * BLOCK:(c + 1) * BLOCK]


def _cast_rows(rows, steps):
    for n in range(steps, 0, -1):
        if rows % n == 0 and (rows // n) % (2 * SUBLANES) == 0:
            return rows // n
    raise ValueError(f"no row split of {rows} over {steps} steps")


def _attention(q, k, v, sinks, weights, batch, seq, q_tile):
    t = q.shape[0]
    tiles = seq // q_tile
    steps = batch * tiles
    cast_specs = []
    for w in weights:
        rp = _cast_rows(w.shape[0], steps)
        last = w.shape[0] // rp - 1
        cast_specs.append(pl.BlockSpec((rp, w.shape[1]),
                                       lambda b, i, s, last=last: (jnp.minimum(b * tiles + i, last), 0)))
    row = pl.BlockSpec((q_tile, ATTN_W), lambda b, i, s: (b * tiles + i, 0))
    kv = pl.BlockSpec((seq, KV_W), lambda b, i, s: (b, 0))
    return pl.pallas_call(
        functools.partial(_attn_kernel, q_tile=q_tile, n_casts=len(weights)),
        grid_spec=pltpu.PrefetchScalarGridSpec(
            num_scalar_prefetch=1,
            grid=(batch, tiles),
            in_specs=[row, kv, kv] + cast_specs,
            out_specs=[row] + cast_specs,
        ),
        out_shape=[jax.ShapeDtypeStruct((t, ATTN_W), BF16)] + [jax.ShapeDtypeStruct(w.shape, BF16) for w in weights],
        compiler_params=_params(2),
        name="swa",
    )(sinks, q, k, v, *weights)


def _merge_kernel(x_ref, a_ref, c_ref, ga_ref, gb_ref, wa_ref, wc_ref, wo_ref, o_ref):
    c = _dot(c_ref[...], wc_ref[...])
    a = _dot(a_ref[...], wa_ref[...])
    merged = (ga_ref[...].astype(F32) * a + gb_ref[...].astype(F32) * c).astype(BF16)
    o_ref[...] = x_ref[...] + _dot(merged, wo_ref[...])


def _merge(x, a, c_act, ga, gb, wa, wc, wo, tm):
    t = x.shape[0]
    row = pl.BlockSpec((tm, D_MODEL), lambda i: (i, 0))
    mat = _const_spec((D_MODEL, D_MODEL))
    return pl.pallas_call(
        _merge_kernel,
        grid=(t // tm,),
        in_specs=[row, row, row, row, row, mat, mat, mat],
        out_specs=row,
        out_shape=jax.ShapeDtypeStruct((t, D_MODEL), F32),
        compiler_params=_params(1),
        name="merge",
    )(x, a, c_act, ga, gb, wa, wc, wo)


def _rope_tables(seq):
    half = ROPE_DIM // 2
    inv_freq = ROPE_THETA ** (-jnp.arange(0, ROPE_DIM, 2, dtype=F32) / ROPE_DIM)
    ang = jnp.arange(seq, dtype=F32)[:, None] * inv_freq[None, :]
    cos, sin = jnp.cos(ang), jnp.sin(ang)
    d = np.arange(LANES) % HEAD_DIM
    cos_l, sin_l = jnp.ones((seq, LANES), F32), jnp.zeros((seq, LANES), F32)
    for j in range(half):
        pick = (d < ROPE_DIM) & (d % half == j)
        cos_l = jnp.where(pick, cos[:, j:j + 1], cos_l)
        sin_l = jnp.where(pick, sin[:, j:j + 1], sin_l)
    return cos_l, jnp.where(d < half, -sin_l, 0.0), jnp.where(d >= half, sin_l, 0.0)


def _block_ones(width):
    head = np.arange(width) // HEAD_DIM
    return jnp.asarray(head[:, None] == head[None, :], BF16)


def kernel(x, p, ffn1_norm, ffn1_wgu, ffn1_wdown, mix_norm, w_in, q_norm, k_norm, sinks, attn_proj, dw_w, dw_b, conv_ln_g, conv_ln_b, conv_proj, w_out, ffn2_norm, ffn2_wgu, ffn2_wdown, ple_proj, ple_norm, ple_gate_norm, ple_gate_w):
    batch, seq, _ = x.shape
    depth = ffn1_norm.shape[0]
    t = batch * seq
    xt = x.reshape(t, D_MODEL)
    c_tab, s_fwd, s_bwd = _rope_tables(seq)
    ones_q, ones_k = _block_ones(2 * LANES), _block_ones(KV_W)
    vec = lambda a: a.reshape(1, -1)
    bf = lambda a: a.astype(BF16)
    for i in range(depth):
        xt, q, k, v, ga, gb, c_act = _front(
            xt, vec(ffn1_norm[i]), bf(ffn1_wgu[i]), bf(ffn1_wdown[i]), vec(mix_norm[i]), bf(w_in[i]),
            ones_q, ones_k, vec(jnp.tile(q_norm[i], N_Q_HEADS)), vec(jnp.tile(k_norm[i], N_KV_HEADS)),
            c_tab, s_fwd, s_bwd, dw_w[i], dw_b[i], vec(conv_ln_g[i]), vec(conv_ln_b[i]), FRONT_TILE, seq)
        later = [attn_proj[i], conv_proj[i], w_out[i], ffn2_wgu[i], ffn2_wdown[i], ple_proj[i], ple_gate_w[i]]
        a, wa, wc, wo, wgu2, wd2, wple, wgate = _attention(q, k, v, sinks[i], later, batch, seq, TILE)
        xt = _merge(xt, a, c_act, ga, gb, wa, wc, wo, TILE)
        xt = _ffn_ple(xt, p[i].reshape(t, PLE_DIM), vec(ffn2_norm[i]), wgu2, wd2,
                      wple, vec(ple_norm[i]), vec(ple_gate_norm[i]), wgate, TILE)
    return xt.reshape(batch, seq, D_MODEL)
```

```python
import functools

import jax
import jax.numpy as jnp
import numpy as np
from jax import lax
from jax.experimental import pallas as pl
from jax.experimental.pallas import tpu as pltpu

D_MODEL = 1024
N_Q_HEADS = 16
N_KV_HEADS = 2
HEAD_DIM = 64
WINDOW = 128
BLOCK = 128
ROPE_THETA = 500000.0
ROPE_DIM = HEAD_DIM // 4
ATTN_W = N_Q_HEADS * HEAD_DIM
KV_W = N_KV_HEADS * HEAD_DIM
CONV_WIDTH = 31
D_FF = 2816
PLE_DIM = 256
EPS = 1e-6
NEG_INF = -1e30
IN_K = ATTN_W
IN_C = IN_K + 2 * KV_W
IN_G = IN_C + 2 * D_MODEL
LOG2E = 1.4426950408889634

LANES = 128
SUBLANES = 8
assert D_MODEL == SUBLANES * LANES
CONV_HALO = 32
CONV_STEPS = 8
LN_ROWS = 32
MERGE_CHUNK = 256
FF_CHUNK = 256
assert D_FF % FF_CHUNK == 0
VMEM_LIMIT = 60 * 1024 * 1024
FRONT_TILE = 512
TILE = 512

BF16 = jnp.bfloat16
F32 = jnp.float32


def _dot(a, b):
    return jnp.dot(a, b, preferred_element_type=F32)


def _rmsnorm(x, g):
    return x * lax.rsqrt(jnp.mean(x * x, axis=-1, keepdims=True) + EPS) * g


def _const_spec(shape):
    return pl.BlockSpec(shape, lambda *_: (0,) * len(shape), pipeline_mode=pl.Buffered(1))


def _params(n_axes, flags=None):
    return pltpu.CompilerParams(dimension_semantics=("arbitrary",) * n_axes,
                                vmem_limit_bytes=VMEM_LIMIT, flags=flags)


def _swiglu_update(x, g_ref, wgu_ref, wd_ref):
    h = _rmsnorm(x, g_ref[...]).astype(BF16)
    g = _dot(h, wgu_ref[:, :D_FF])
    u = _dot(h, wgu_ref[:, D_FF:])
    a = (g * jax.nn.sigmoid(g) * u).astype(BF16)
    return x + 0.5 * _dot(a, wd_ref[...])


def _zero_after(values):
    tiles = [v[r:r + SUBLANES, c:c + LANES] for v in values
             for r in range(0, v.shape[0], SUBLANES) for c in range(0, v.shape[1], LANES)]
    half_word = jnp.uint32(16)
    bits = functools.reduce(jnp.bitwise_or, [pltpu.bitcast(t, jnp.uint32) for t in tiles])
    return lax.shift_right_logical(lax.shift_right_logical(bits, half_word), half_word).astype(F32)


class _Weaver:
    def __init__(self, lhs_ref, piece):
        self.lhs_ref, self.piece, self.zero = lhs_ref, piece, None

    def after_matmul(self, n=1):
        tile = (pl.ds(0, 2 * SUBLANES), pl.ds(0, LANES))
        if self.zero is not None:
            z = jnp.concatenate([self.zero, self.zero], axis=0)
            self.lhs_ref[tile] = (self.lhs_ref[tile].astype(F32) + z).astype(BF16)
        results = []
        for _ in range(n):
            results.extend(self.piece() or [])
        self.zero = _zero_after(results) if results else None


def _swiglu_update_chunked(x, g_ref, wgu_ref, wd_ref, weaver, n_pieces):
    h_ref = weaver.lhs_ref
    h_ref[...] = _rmsnorm(x, g_ref[...]).astype(BF16)
    y = None
    for c in range(0, D_FF, FF_CHUNK):
        g = _dot(h_ref[...], wgu_ref[:, c:c + FF_CHUNK])
        weaver.after_matmul(n_pieces)
        u = _dot(h_ref[...], wgu_ref[:, D_FF + c:D_FF + c + FF_CHUNK])
        weaver.after_matmul(n_pieces)
        a = (g * jax.nn.sigmoid(g) * u).astype(BF16)
        d = _dot(a, wd_ref[c:c + FF_CHUNK, :])
        weaver.after_matmul(n_pieces)
        y = d if y is None else y + d
    return x + 0.5 * y


def _ffn_ple_kernel(x_ref, p_ref, g_ref, wgu_ref, wd_ref, pw_ref, pn_ref, gn_ref, gw_ref, o_ref):
    x = _swiglu_update(x_ref[...], g_ref, wgu_ref, wd_ref)
    e = _rmsnorm(_dot(p_ref[...].astype(BF16), pw_ref[...]), pn_ref[...])
    gate = jax.nn.sigmoid(_dot(_rmsnorm(x, gn_ref[...]).astype(BF16), gw_ref[...]))
    o_ref[...] = x + gate * e


def _ffn_ple(x, p, norm, wgu, wd, ple_w, ple_norm, gate_norm, gate_w, tm):
    t = x.shape[0]
    row = pl.BlockSpec((tm, D_MODEL), lambda i: (i, 0))
    vec = _const_spec((1, D_MODEL))
    return pl.pallas_call(
        _ffn_ple_kernel,
        grid=(t // tm,),
        in_specs=[row, pl.BlockSpec((tm, PLE_DIM), lambda i: (i, 0)), vec,
                  _const_spec((D_MODEL, 2 * D_FF)), _const_spec((D_FF, D_MODEL)),
                  _const_spec((PLE_DIM, D_MODEL)), vec, vec, _const_spec((D_MODEL, D_MODEL))],
        out_specs=row,
        out_shape=jax.ShapeDtypeStruct((t, D_MODEL), F32),
        compiler_params=_params(1),
        name="ffn2_ple",
    )(x, p, norm, wgu, wd, ple_w, ple_norm, gate_norm, gate_w)


def _head_norm_rope(z, ones_blk, gain, c_tab, s_fwd, s_bwd):
    w = z.shape[1]
    blk = ones_blk.shape[0]
    sq = (z * z).astype(BF16)
    ss = jnp.concatenate([_dot(sq[:, i:i + blk], ones_blk) for i in range(0, w, blk)], axis=1)
    t = z * lax.rsqrt(ss * (1.0 / HEAD_DIM) + EPS) * gain
    half = ROPE_DIM // 2
    up = pltpu.roll(t, w - half, axis=1)
    dn = pltpu.roll(t, half, axis=1)
    reps = w // LANES
    tile = lambda a: jnp.concatenate([a] * reps, axis=1) if reps > 1 else a
    return t * tile(c_tab) + up * tile(s_fwd) + dn * tile(s_bwd)


def _conv_ln_silu(tmaj_ref, y_ref, dw_ref, db_ref, lg_ref, lb_ref, out_ref, tm):
    base = CONV_HALO - (CONV_WIDTH - 1)

    def conv_piece(t0):
        acc = [None] * CONV_STEPS
        for j in range(CONV_WIDTH):
            w_j = dw_ref[pl.ds(j * SUBLANES, SUBLANES), :]
            for r in range(CONV_STEPS):
                v = tmaj_ref[pl.ds((t0 + r + base + j) * SUBLANES, SUBLANES), :] * w_j
                acc[r] = v if j == 0 else acc[r] + v
        bias = db_ref[...]
        acc = [a + bias for a in acc]
        for r in range(CONV_STEPS):
            y_ref[pl.ds((t0 + r) * SUBLANES, SUBLANES), :] = acc[r]
        return acc

    def ln_piece(t0):
        y = jnp.concatenate([y_ref[pl.ds(t0 * SUBLANES + c, LN_ROWS, stride=SUBLANES), :]
                             for c in range(SUBLANES)], axis=1)
        yc = y - jnp.mean(y, axis=-1, keepdims=True)
        var = jnp.mean(yc * yc, axis=-1, keepdims=True)
        ln = yc * lax.rsqrt(var + EPS) * lg_ref[...] + lb_ref[...]
        act = ln * jax.nn.sigmoid(ln)
        out_ref[pl.ds(t0, LN_ROWS), :] = act.astype(BF16)
        return [act]

    pending = [functools.partial(conv_piece, t0) for t0 in range(0, tm, CONV_STEPS)]
    pending += [functools.partial(ln_piece, t0) for t0 in range(0, tm, LN_ROWS)]

    def piece():
        return pending.pop(0)() if pending else None

    def finish():
        while pending:
            piece()

    return piece, finish


def _front_kernel(x_ref, n1_ref, wgu_ref, wd_ref, nm_ref, win_ref,
                  oq_ref, ok_ref, gq_ref, gk_ref, c_ref, sf_ref, sb_ref, dw_ref, db_ref, lg_ref, lb_ref,
                  x1_out, q_out, k_out, v_out, ga_out, gb_out, cact_out, tmaj_ref, y_ref, h_ref, *, tm, tiles_per_seq):
    step = pl.program_id(0)
    halo_rows = CONV_HALO * SUBLANES

    @pl.when(step == 0)
    def _():
        tmaj_ref[...] = jnp.zeros(tmaj_ref.shape, F32)

    conv = functools.partial(_conv_ln_silu, tmaj_ref, y_ref, dw_ref, db_ref, lg_ref, lb_ref, cact_out, tm)
    last = pl.num_programs(0) - 1

    @pl.when(step == last)
    def _():
        conv()[1]()

    @pl.when(step < last)
    def _():
        conv_piece, conv_finish = conv()
        weaver = _Weaver(h_ref, conv_piece)
        ffn_matmuls = 3 * (D_FF // FF_CHUNK)
        x1 = _swiglu_update_chunked(x_ref[...], n1_ref, wgu_ref, wd_ref, weaver,
                                    pl.cdiv(tm // CONV_STEPS, ffn_matmuls))
        k = tm // (8 * LN_ROWS)
        x1_out[...] = x1
        h_ref[...] = _rmsnorm(x1, nm_ref[...]).astype(BF16)
        c_tab, s_fwd, s_bwd = c_ref[...], sf_ref[...], sb_ref[...]
        zq = _dot(h_ref[...], win_ref[:, :IN_K])
        weaver.after_matmul(k)
        scale = HEAD_DIM ** -0.5 * LOG2E
        q_out[...] = (_head_norm_rope(zq, oq_ref[...], gq_ref[...], c_tab, s_fwd, s_bwd) * scale).astype(BF16)
        zkv = _dot(h_ref[...], win_ref[:, IN_K:IN_C])
        weaver.after_matmul(k)
        k_out[...] = _head_norm_rope(zkv[:, :KV_W], ok_ref[...], gk_ref[...], c_tab, s_fwd, s_bwd).astype(BF16)
        v_out[...] = zkv[:, KV_W:].astype(BF16)
        za = _dot(h_ref[...], win_ref[:, IN_G:IN_G + D_MODEL])
        weaver.after_matmul(2 * k)
        ga_out[...] = jax.nn.sigmoid(za).astype(BF16)
        zb = _dot(h_ref[...], win_ref[:, IN_G + D_MODEL:])
        weaver.after_matmul(2 * k)
        gb_out[...] = jax.nn.sigmoid(zb).astype(BF16)
        u_lin = _dot(h_ref[...], win_ref[:, IN_C:IN_C + D_MODEL])
        weaver.after_matmul(2 * k)
        conv_finish()
        u = u_lin * jax.nn.sigmoid(_dot(h_ref[...], win_ref[:, IN_C + D_MODEL:IN_G]))

        tail = tmaj_ref[pl.ds(tm * SUBLANES, halo_rows), :]
        tmaj_ref[pl.ds(0, halo_rows), :] = jnp.where(step % tiles_per_seq == 0, jnp.zeros_like(tail), tail)
        for c in range(SUBLANES):
            tmaj_ref[pl.ds(halo_rows + c, tm, stride=SUBLANES), :] = u[:, c * LANES:(c + 1) * LANES]


def _front(x, n1, wgu, wd, nm, w_in, ones_q, ones_k, gq, gk, c_tab, s_fwd, s_bwd,
           dw_w, dw_b, ln_g, ln_b, tm, seq):
    t = x.shape[0]
    n_tiles = t // tm
    tiles_per_seq = seq // tm
    cur = lambda s: jnp.minimum(s, n_tiles - 1)
    row = lambda w: pl.BlockSpec((tm, w), lambda s: (cur(s), 0))
    lag = pl.BlockSpec((tm, D_MODEL), lambda s: (jnp.maximum(s - 1, 0), 0))
    tab = pl.BlockSpec((tm, LANES), lambda s: (cur(s) % tiles_per_seq, 0))
    vec = _const_spec((1, D_MODEL))
    out = lambda w, dt=BF16: jax.ShapeDtypeStruct((t, w), dt)
    dw_tmaj = dw_w.reshape(CONV_WIDTH * SUBLANES, LANES)
    db_tmaj = dw_b.reshape(SUBLANES, LANES)
    return pl.pallas_call(
        functools.partial(_front_kernel, tm=tm, tiles_per_seq=tiles_per_seq),
        grid=(n_tiles + 1,),
        in_specs=[row(D_MODEL), vec, _const_spec(wgu.shape), _const_spec(wd.shape), vec, _const_spec(w_in.shape),
                  _const_spec(ones_q.shape), _const_spec(ones_k.shape),
                  _const_spec((1, ATTN_W)), _const_spec((1, KV_W)), tab, tab, tab,
                  _const_spec(dw_tmaj.shape), _const_spec(db_tmaj.shape), vec, vec],
        out_specs=[row(D_MODEL), row(ATTN_W), row(KV_W), row(KV_W), row(D_MODEL), row(D_MODEL), lag],
        out_shape=[out(D_MODEL, F32), out(ATTN_W), out(KV_W), out(KV_W), out(D_MODEL), out(D_MODEL),
                   out(D_MODEL)],
        scratch_shapes=[pltpu.VMEM(((CONV_HALO + tm) * SUBLANES, LANES), F32),
                        pltpu.VMEM((tm * SUBLANES, LANES), F32),
                        pltpu.VMEM((tm, D_MODEL), BF16)],
        compiler_params=_params(1),
        name="ffn1_proj_conv",
    )(x, n1, wgu, wd, nm, w_in, ones_q, ones_k, gq, gk, c_tab, s_fwd, s_bwd,
      dw_tmaj, db_tmaj, ln_g, ln_b)


def _attn_merge_kernel(sinks_ref, q_ref, k_ref, v_ref, x_ref, c_ref, ga_ref, gb_ref, wa_ref, wc_ref, wo_ref,
                       *rest, q_tile, n_casts):
    o_ref = rest[n_casts]
    attn_ref, wa_s, wc_s, wo_s = rest[2 * n_casts + 1:]
    for src, dst in zip(rest[:n_casts], rest[n_casts + 1:2 * n_casts + 1]):
        dst[...] = src[...].astype(BF16)

    @pl.when((pl.program_id(0) == 0) & (pl.program_id(1) == 0))
    def _():
        for w_ref, w_s in ((wa_ref, wa_s), (wc_ref, wc_s), (wo_ref, wo_s)):
            w_s[...] = w_ref[...].astype(BF16)

    tile_start = pl.program_id(1) * q_tile
    lane = lax.broadcasted_iota(jnp.int32, (2 * BLOCK, LANES), 1)
    lo = lane < HEAD_DIM
    tiles_per_kv = ATTN_W // LANES // N_KV_HEADS
    qi = lax.broadcasted_iota(jnp.int32, (tiles_per_kv * BLOCK, 2 * BLOCK), 0) % BLOCK
    kj = lax.broadcasted_iota(jnp.int32, (tiles_per_kv * BLOCK, 2 * BLOCK), 1)
    delta = qi + BLOCK - kj
    band = (delta >= 0) & (delta < WINDOW)
    out_lo = lax.broadcasted_iota(jnp.int32, (tiles_per_kv * BLOCK, LANES), 1) < HEAD_DIM
    key_lo = lax.broadcasted_iota(jnp.int32, (2 * 2 * BLOCK, LANES), 0) < 2 * BLOCK
    ones_hat = (key_lo == out_lo).astype(BF16)

    c_parts, pending_zero = [], None
    for blk in range(q_tile // BLOCK):
        start = pl.multiple_of(tile_start + blk * BLOCK, BLOCK)
        prev = pl.multiple_of(jnp.maximum(start - BLOCK, 0), BLOCK)
        valid = band & (start + kj - BLOCK >= 0) if blk == 0 else band
        rows = pl.ds(blk * BLOCK, BLOCK)

        def band_of(ref):
            return jnp.concatenate([ref[pl.ds(prev, BLOCK), :], ref[pl.ds(start, BLOCK), :]], axis=0)

        kf, vf = band_of(k_ref), band_of(v_ref)
        kr, vr = pltpu.roll(kf, HEAD_DIM, axis=1), pltpu.roll(vf, HEAD_DIM, axis=1)
        zero = jnp.zeros_like(kf)
        for h in range(N_KV_HEADS):
            k_a, k_b = (kf, kr) if h == 0 else (kr, kf)
            v_a, v_b = (vf, vr) if h == 0 else (vr, vf)
            k_top = jnp.where(lo, k_a, zero)
            if pending_zero is not None:
                z = jnp.concatenate([pending_zero, pending_zero], axis=0).astype(BF16)
                k_top = jnp.concatenate([k_top[:2 * SUBLANES] + z, k_top[2 * SUBLANES:]], axis=0)
                pending_zero = None
            k_hat = jnp.concatenate([k_top, jnp.where(lo, zero, k_b)], axis=0)
            v_hat = jnp.concatenate([jnp.where(lo, v_a, zero), jnp.where(lo, zero, v_b)], axis=0)
            q_h = jnp.concatenate(
                [q_ref[rows, pl.ds((h * tiles_per_kv + c) * LANES, LANES)] for c in range(tiles_per_kv)], axis=0)
            s = lax.dot_general(q_h, k_hat, (((1,), (1,)), ((), ())), preferred_element_type=F32)
            probs, sink_terms = [], []
            for half in range(2):
                sh = jnp.where(valid, s[:, half * 2 * BLOCK:(half + 1) * 2 * BLOCK], NEG_INF)
                sink = jnp.concatenate(
                    [jnp.full((BLOCK, LANES), sinks_ref[2 * (h * tiles_per_kv + c) + half] * LOG2E, F32)
                     for c in range(tiles_per_kv)], axis=0)
                m = jnp.max(jnp.maximum(jnp.maximum(sh[:, :LANES], sh[:, LANES:]), sink), axis=-1, keepdims=True)
                probs.append(jnp.exp2(sh - m).astype(BF16))
                sink_terms.append(jnp.exp2(sink - m))
            o_aug = _dot(jnp.concatenate(probs, axis=1), jnp.concatenate([v_hat, ones_hat], axis=1))
            denom = o_aug[:, LANES:] + jnp.where(out_lo, sink_terms[0], sink_terms[1])
            o2 = (o_aug[:, :LANES] / denom).astype(BF16)
            for c in range(tiles_per_kv):
                attn_ref[rows, pl.ds((h * tiles_per_kv + c) * LANES, LANES)] = o2[c * BLOCK:(c + 1) * BLOCK]
            if len(c_parts) * MERGE_CHUNK < D_MODEL:
                col = len(c_parts) * MERGE_CHUNK
                c_parts.append(_dot(c_ref[...], wc_s[:, col:col + MERGE_CHUNK]))
                pending_zero = _zero_after([c_parts[-1][-SUBLANES:, -LANES:]])

    while len(c_parts) * MERGE_CHUNK < D_MODEL:
        col = len(c_parts) * MERGE_CHUNK
        c_parts.append(_dot(c_ref[...], wc_s[:, col:col + MERGE_CHUNK]))
    c = jnp.concatenate(c_parts, axis=1)
    a = _dot(attn_ref[...], wa_s[...])
    merged = (ga_ref[...].astype(F32) * a + gb_ref[...].astype(F32) * c).astype(BF16)
    o_ref[...] = x_ref[...] + _dot(merged, wo_s[...])


def _cast_rows(rows, steps):
    for n in range(steps, 0, -1):
        if rows % n == 0 and (rows // n) % (2 * SUBLANES) == 0:
            return rows // n
    raise ValueError(f"no row split of {rows} over {steps} steps")


def _attention_merge(x, q, k, v, sinks, c_act, ga, gb, wa, wc, wo, weights, batch, seq, q_tile):
    t = q.shape[0]
    tiles = seq // q_tile
    steps = batch * tiles
    cast_specs = []
    for w in weights:
        rp = _cast_rows(w.shape[0], steps)
        last = w.shape[0] // rp - 1
        cast_specs.append(pl.BlockSpec((rp, w.shape[1]),
                                       lambda b, i, s, last=last: (jnp.minimum(b * tiles + i, last), 0)))
    row = pl.BlockSpec((q_tile, D_MODEL), lambda b, i, s: (b * tiles + i, 0))
    kv = pl.BlockSpec((seq, KV_W), lambda b, i, s: (b, 0))
    mat = _const_spec((D_MODEL, D_MODEL))
    return pl.pallas_call(
        functools.partial(_attn_merge_kernel, q_tile=q_tile, n_casts=len(weights)),
        grid_spec=pltpu.PrefetchScalarGridSpec(
            num_scalar_prefetch=1,
            grid=(batch, tiles),
            in_specs=[row, kv, kv, row, row, row, row, mat, mat, mat] + cast_specs,
            out_specs=[row] + cast_specs,
            scratch_shapes=[pltpu.VMEM((q_tile, ATTN_W), BF16)] + [pltpu.VMEM((D_MODEL, D_MODEL), BF16)] * 3,
        ),
        out_shape=[jax.ShapeDtypeStruct((t, D_MODEL), F32)] + [jax.ShapeDtypeStruct(w.shape, BF16) for w in weights],
        compiler_params=_params(2),
        name="swa_merge",
    )(sinks, q, k, v, x, c_act, ga, gb, wa, wc, wo, *weights)


def _rope_tables(seq):
    half = ROPE_DIM // 2
    inv_freq = ROPE_THETA ** (-jnp.arange(0, ROPE_DIM, 2, dtype=F32) / ROPE_DIM)
    ang = jnp.arange(seq, dtype=F32)[:, None] * inv_freq[None, :]
    cos, sin = jnp.cos(ang), jnp.sin(ang)
    d = np.arange(LANES) % HEAD_DIM
    cos_l, sin_l = jnp.ones((seq, LANES), F32), jnp.zeros((seq, LANES), F32)
    for j in range(half):
        pick = (d < ROPE_DIM) & (d % half == j)
        cos_l = jnp.where(pick, cos[:, j:j + 1], cos_l)
        sin_l = jnp.where(pick, sin[:, j:j + 1], sin_l)
    return cos_l, jnp.where(d < half, -sin_l, 0.0), jnp.where(d >= half, sin_l, 0.0)


def _block_ones(width):
    head = np.arange(width) // HEAD_DIM
    return jnp.asarray(head[:, None] == head[None, :], BF16)


def kernel(x, p, ffn1_norm, ffn1_wgu, ffn1_wdown, mix_norm, w_in, q_norm, k_norm, sinks, attn_proj, dw_w, dw_b, conv_ln_g, conv_ln_b, conv_proj, w_out, ffn2_norm, ffn2_wgu, ffn2_wdown, ple_proj, ple_norm, ple_gate_norm, ple_gate_w):
    batch, seq, _ = x.shape
    depth = ffn1_norm.shape[0]
    t = batch * seq
    xt = x.reshape(t, D_MODEL)
    c_tab, s_fwd, s_bwd = _rope_tables(seq)
    ones_q, ones_k = _block_ones(2 * LANES), _block_ones(KV_W)
    vec = lambda a: a.reshape(1, -1)
    bf = lambda a: a.astype(BF16)
    for i in range(depth):
        xt, q, k, v, ga, gb, c_act = _front(
            xt, vec(ffn1_norm[i]), bf(ffn1_wgu[i]), bf(ffn1_wdown[i]), vec(mix_norm[i]), bf(w_in[i]),
            ones_q, ones_k, vec(jnp.tile(q_norm[i], N_Q_HEADS)), vec(jnp.tile(k_norm[i], N_KV_HEADS)),
            c_tab, s_fwd, s_bwd, dw_w[i], dw_b[i], vec(conv_ln_g[i]), vec(conv_ln_b[i]), FRONT_TILE, seq)
        later = [ffn2_wgu[i], ffn2_wdown[i], ple_proj[i], ple_gate_w[i]]
        xt, wgu2, wd2, wple, wgate = _attention_merge(xt, q, k, v, sinks[i], c_act, ga, gb, attn_proj[i],
                                                      conv_proj[i], w_out[i], later, batch, seq, TILE)
        xt = _ffn_ple(xt, p[i].reshape(t, PLE_DIM), vec(ffn2_norm[i]), wgu2, wd2,
                      wple, vec(ple_norm[i]), vec(ple_gate_norm[i]), wgate, TILE)
    return xt.reshape(batch, seq, D_MODEL)
```

```python
import functools

import jax
import jax.numpy as jnp
import numpy as np
from jax import lax
from jax.experimental import pallas as pl
from jax.experimental.pallas import tpu as pltpu

D_MODEL = 1024
N_Q_HEADS = 16
N_KV_HEADS = 2
HEAD_DIM = 64
WINDOW = 128
BLOCK = 128
ROPE_THETA = 500000.0
ROPE_DIM = HEAD_DIM // 4
ATTN_W = N_Q_HEADS * HEAD_DIM
KV_W = N_KV_HEADS * HEAD_DIM
CONV_WIDTH = 31
D_FF = 2816
PLE_DIM = 256
EPS = 1e-6
NEG_INF = -1e30
IN_K = ATTN_W
IN_C = IN_K + 2 * KV_W
IN_G = IN_C + 2 * D_MODEL
LOG2E = 1.4426950408889634

LANES = 128
SUBLANES = 8
assert D_MODEL == SUBLANES * LANES
CONV_HALO = 32
CONV_STEPS = 8
LN_ROWS = 32
MERGE_CHUNK = 256
FF_CHUNK = 512
VMEM_LIMIT = 60 * 1024 * 1024
FRONT_TILE = 512
TILE = 512

BF16 = jnp.bfloat16
F32 = jnp.float32


def _dot(a, b):
    return jnp.dot(a, b, preferred_element_type=F32)


def _rmsnorm(x, g):
    return x * lax.rsqrt(jnp.mean(x * x, axis=-1, keepdims=True) + EPS) * g


def _const_spec(shape):
    return pl.BlockSpec(shape, lambda *_: (0,) * len(shape), pipeline_mode=pl.Buffered(1))


def _params(n_axes, flags=None):
    return pltpu.CompilerParams(dimension_semantics=("arbitrary",) * n_axes,
                                vmem_limit_bytes=VMEM_LIMIT, flags=flags)


def _swiglu_update(x, g_ref, wgu_ref, wd_ref):
    h = _rmsnorm(x, g_ref[...]).astype(BF16)
    g = _dot(h, wgu_ref[:, :D_FF])
    u = _dot(h, wgu_ref[:, D_FF:])
    a = (g * jax.nn.sigmoid(g) * u).astype(BF16)
    return x + 0.5 * _dot(a, wd_ref[...])


def _zero_after(values):
    tiles = [v[r:r + SUBLANES, c:c + LANES] for v in values
             for r in range(0, v.shape[0], SUBLANES) for c in range(0, v.shape[1], LANES)]
    half_word = jnp.uint32(16)
    bits = functools.reduce(jnp.bitwise_or, [pltpu.bitcast(t, jnp.uint32) for t in tiles])
    return lax.shift_right_logical(lax.shift_right_logical(bits, half_word), half_word).astype(F32)


class _Weaver:
    def __init__(self, lhs_ref, piece):
        self.lhs_ref, self.piece, self.zero = lhs_ref, piece, None

    def after_matmul(self, n=1):
        tile = (pl.ds(0, 2 * SUBLANES), pl.ds(0, LANES))
        if self.zero is not None:
            z = jnp.concatenate([self.zero, self.zero], axis=0)
            self.lhs_ref[tile] = (self.lhs_ref[tile].astype(F32) + z).astype(BF16)
        results = []
        for _ in range(n):
            results.extend(self.piece() or [])
        self.zero = _zero_after(results) if results else None


def _swiglu_update_chunked(x, g_ref, wgu_ref, wd_ref, weaver, n_pieces):
    h_ref = weaver.lhs_ref
    h_ref[...] = _rmsnorm(x, g_ref[...]).astype(BF16)
    y = None
    for c in range(0, D_FF, FF_CHUNK):
        w = min(FF_CHUNK, D_FF - c)
        g = _dot(h_ref[...], wgu_ref[:, c:c + w])
        weaver.after_matmul(n_pieces)
        u = _dot(h_ref[...], wgu_ref[:, D_FF + c:D_FF + c + w])
        weaver.after_matmul(n_pieces)
        a = (g * jax.nn.sigmoid(g) * u).astype(BF16)
        d = _dot(a, wd_ref[c:c + w, :])
        weaver.after_matmul(n_pieces)
        y = d if y is None else y + d
    return x + 0.5 * y


def _ffn_ple_kernel(x_ref, p_ref, g_ref, wgu_ref, wd_ref, pw_ref, pn_ref, gn_ref, gw_ref, o_ref):
    x = _swiglu_update(x_ref[...], g_ref, wgu_ref, wd_ref)
    e = _rmsnorm(_dot(p_ref[...].astype(BF16), pw_ref[...]), pn_ref[...])
    gate = jax.nn.sigmoid(_dot(_rmsnorm(x, gn_ref[...]).astype(BF16), gw_ref[...]))
    o_ref[...] = x + gate * e


def _ffn_ple(x, p, norm, wgu, wd, ple_w, ple_norm, gate_norm, gate_w, tm):
    t = x.shape[0]
    row = pl.BlockSpec((tm, D_MODEL), lambda i: (i, 0))
    vec = _const_spec((1, D_MODEL))
    return pl.pallas_call(
        _ffn_ple_kernel,
        grid=(t // tm,),
        in_specs=[row, pl.BlockSpec((tm, PLE_DIM), lambda i: (i, 0)), vec,
                  _const_spec((D_MODEL, 2 * D_FF)), _const_spec((D_FF, D_MODEL)),
                  _const_spec((PLE_DIM, D_MODEL)), vec, vec, _const_spec((D_MODEL, D_MODEL))],
        out_specs=row,
        out_shape=jax.ShapeDtypeStruct((t, D_MODEL), F32),
        compiler_params=_params(1),
        name="ffn2_ple",
    )(x, p, norm, wgu, wd, ple_w, ple_norm, gate_norm, gate_w)


def _head_norm_rope(z, ones_blk, gain, c_tab, s_fwd, s_bwd):
    w = z.shape[1]
    blk = ones_blk.shape[0]
    sq = (z * z).astype(BF16)
    ss = jnp.concatenate([_dot(sq[:, i:i + blk], ones_blk) for i in range(0, w, blk)], axis=1)
    t = z * lax.rsqrt(ss * (1.0 / HEAD_DIM) + EPS) * gain
    half = ROPE_DIM // 2
    up = pltpu.roll(t, w - half, axis=1)
    dn = pltpu.roll(t, half, axis=1)
    reps = w // LANES
    tile = lambda a: jnp.concatenate([a] * reps, axis=1) if reps > 1 else a
    return t * tile(c_tab) + up * tile(s_fwd) + dn * tile(s_bwd)


def _conv_ln_silu(tmaj_ref, y_ref, dw_ref, db_ref, lg_ref, lb_ref, out_ref, tm):
    base = CONV_HALO - (CONV_WIDTH - 1)

    def conv_piece(t0):
        acc = [None] * CONV_STEPS
        for j in range(CONV_WIDTH):
            w_j = dw_ref[pl.ds(j * SUBLANES, SUBLANES), :]
            for r in range(CONV_STEPS):
                v = tmaj_ref[pl.ds((t0 + r + base + j) * SUBLANES, SUBLANES), :] * w_j
                acc[r] = v if j == 0 else acc[r] + v
        bias = db_ref[...]
        acc = [a + bias for a in acc]
        for r in range(CONV_STEPS):
            y_ref[pl.ds((t0 + r) * SUBLANES, SUBLANES), :] = acc[r]
        return acc

    def ln_piece(t0):
        y = jnp.concatenate([y_ref[pl.ds(t0 * SUBLANES + c, LN_ROWS, stride=SUBLANES), :]
                             for c in range(SUBLANES)], axis=1)
        yc = y - jnp.mean(y, axis=-1, keepdims=True)
        var = jnp.mean(yc * yc, axis=-1, keepdims=True)
        ln = yc * lax.rsqrt(var + EPS) * lg_ref[...] + lb_ref[...]
        act = ln * jax.nn.sigmoid(ln)
        out_ref[pl.ds(t0, LN_ROWS), :] = act.astype(BF16)
        return [act]

    pending = [functools.partial(conv_piece, t0) for t0 in range(0, tm, CONV_STEPS)]
    pending += [functools.partial(ln_piece, t0) for t0 in range(0, tm, LN_ROWS)]

    def piece():
        return pending.pop(0)() if pending else None

    def finish():
        while pending:
            piece()

    return piece, finish


def _front_kernel(x_ref, n1_ref, wgu_ref, wd_ref, nm_ref, win_ref,
                  oq_ref, ok_ref, gq_ref, gk_ref, c_ref, sf_ref, sb_ref, dw_ref, db_ref, lg_ref, lb_ref,
                  x1_out, q_out, k_out, v_out, ga_out, gb_out, cact_out, tmaj_ref, y_ref, h_ref, *, tm, tiles_per_seq):
    step = pl.program_id(0)
    halo_rows = CONV_HALO * SUBLANES

    @pl.when(step == 0)
    def _():
        tmaj_ref[...] = jnp.zeros(tmaj_ref.shape, F32)

    conv = functools.partial(_conv_ln_silu, tmaj_ref, y_ref, dw_ref, db_ref, lg_ref, lb_ref, cact_out, tm)
    last = pl.num_programs(0) - 1

    @pl.when(step == last)
    def _():
        conv()[1]()

    @pl.when(step < last)
    def _():
        conv_piece, conv_finish = conv()
        weaver = _Weaver(h_ref, conv_piece)
        ffn_matmuls = 3 * pl.cdiv(D_FF, FF_CHUNK)
        x1 = _swiglu_update_chunked(x_ref[...], n1_ref, wgu_ref, wd_ref, weaver,
                                    pl.cdiv(tm // CONV_STEPS, ffn_matmuls))
        k = tm // (8 * LN_ROWS)
        x1_out[...] = x1
        h_ref[...] = _rmsnorm(x1, nm_ref[...]).astype(BF16)
        c_tab, s_fwd, s_bwd = c_ref[...], sf_ref[...], sb_ref[...]
        zq = _dot(h_ref[...], win_ref[:, :IN_K])
        weaver.after_matmul(k)
        scale = HEAD_DIM ** -0.5 * LOG2E
        q_out[...] = (_head_norm_rope(zq, oq_ref[...], gq_ref[...], c_tab, s_fwd, s_bwd) * scale).astype(BF16)
        zkv = _dot(h_ref[...], win_ref[:, IN_K:IN_C])
        weaver.after_matmul(k)
        k_out[...] = _head_norm_rope(zkv[:, :KV_W], ok_ref[...], gk_ref[...], c_tab, s_fwd, s_bwd).astype(BF16)
        v_out[...] = zkv[:, KV_W:].astype(BF16)
        za = _dot(h_ref[...], win_ref[:, IN_G:IN_G + D_MODEL])
        weaver.after_matmul(2 * k)
        ga_out[...] = jax.nn.sigmoid(za).astype(BF16)
        zb = _dot(h_ref[...], win_ref[:, IN_G + D_MODEL:])
        weaver.after_matmul(2 * k)
        gb_out[...] = jax.nn.sigmoid(zb).astype(BF16)
        u_lin = _dot(h_ref[...], win_ref[:, IN_C:IN_C + D_MODEL])
        weaver.after_matmul(2 * k)
        conv_finish()
        u = u_lin * jax.nn.sigmoid(_dot(h_ref[...], win_ref[:, IN_C + D_MODEL:IN_G]))

        tail = tmaj_ref[pl.ds(tm * SUBLANES, halo_rows), :]
        tmaj_ref[pl.ds(0, halo_rows), :] = jnp.where(step % tiles_per_seq == 0, jnp.zeros_like(tail), tail)
        for c in range(SUBLANES):
            tmaj_ref[pl.ds(halo_rows + c, tm, stride=SUBLANES), :] = u[:, c * LANES:(c + 1) * LANES]


def _front(x, n1, wgu, wd, nm, w_in, ones_q, ones_k, gq, gk, c_tab, s_fwd, s_bwd,
           dw_w, dw_b, ln_g, ln_b, tm, seq):
    t = x.shape[0]
    n_tiles = t // tm
    tiles_per_seq = seq // tm
    cur = lambda s: jnp.minimum(s, n_tiles - 1)
    row = lambda w: pl.BlockSpec((tm, w), lambda s: (cur(s), 0))
    lag = pl.BlockSpec((tm, D_MODEL), lambda s: (jnp.maximum(s - 1, 0), 0))
    tab = pl.BlockSpec((tm, LANES), lambda s: (cur(s) % tiles_per_seq, 0))
    vec = _const_spec((1, D_MODEL))
    out = lambda w, dt=BF16: jax.ShapeDtypeStruct((t, w), dt)
    dw_tmaj = dw_w.reshape(CONV_WIDTH * SUBLANES, LANES)
    db_tmaj = dw_b.reshape(SUBLANES, LANES)
    return pl.pallas_call(
        functools.partial(_front_kernel, tm=tm, tiles_per_seq=tiles_per_seq),
        grid=(n_tiles + 1,),
        in_specs=[row(D_MODEL), vec, _const_spec(wgu.shape), _const_spec(wd.shape), vec, _const_spec(w_in.shape),
                  _const_spec(ones_q.shape), _const_spec(ones_k.shape),
                  _const_spec((1, ATTN_W)), _const_spec((1, KV_W)), tab, tab, tab,
                  _const_spec(dw_tmaj.shape), _const_spec(db_tmaj.shape), vec, vec],
        out_specs=[row(D_MODEL), row(ATTN_W), row(KV_W), row(KV_W), row(D_MODEL), row(D_MODEL), lag],
        out_shape=[out(D_MODEL, F32), out(ATTN_W), out(KV_W), out(KV_W), out(D_MODEL), out(D_MODEL),
                   out(D_MODEL)],
        scratch_shapes=[pltpu.VMEM(((CONV_HALO + tm) * SUBLANES, LANES), F32),
                        pltpu.VMEM((tm * SUBLANES, LANES), F32),
                        pltpu.VMEM((tm, D_MODEL), BF16)],
        compiler_params=_params(1),
        name="ffn1_proj_conv",
    )(x, n1, wgu, wd, nm, w_in, ones_q, ones_k, gq, gk, c_tab, s_fwd, s_bwd,
      dw_tmaj, db_tmaj, ln_g, ln_b)


def _attn_merge_kernel(sinks_ref, q_ref, k_ref, v_ref, x_ref, c_ref, ga_ref, gb_ref, wa_ref, wc_ref, wo_ref,
                       *rest, q_tile, n_casts):
    o_ref = rest[n_casts]
    attn_ref, wa_s, wc_s, wo_s = rest[2 * n_casts + 1:]
    for src, dst in zip(rest[:n_casts], rest[n_casts + 1:2 * n_casts + 1]):
        dst[...] = src[...].astype(BF16)

    @pl.when((pl.program_id(0) == 0) & (pl.program_id(1) == 0))
    def _():
        for w_ref, w_s in ((wa_ref, wa_s), (wc_ref, wc_s), (wo_ref, wo_s)):
            w_s[...] = w_ref[...].astype(BF16)

    tile_start = pl.program_id(1) * q_tile
    lane = lax.broadcasted_iota(jnp.int32, (2 * BLOCK, LANES), 1)
    lo = lane < HEAD_DIM
    tiles_per_kv = ATTN_W // LANES // N_KV_HEADS
    qi = lax.broadcasted_iota(jnp.int32, (tiles_per_kv * BLOCK, 2 * BLOCK), 0) % BLOCK
    kj = lax.broadcasted_iota(jnp.int32, (tiles_per_kv * BLOCK, 2 * BLOCK), 1)
    delta = qi + BLOCK - kj
    band = (delta >= 0) & (delta < WINDOW)
    out_lo = lax.broadcasted_iota(jnp.int32, (tiles_per_kv * BLOCK, LANES), 1) < HEAD_DIM
    key_lo = lax.broadcasted_iota(jnp.int32, (2 * 2 * BLOCK, LANES), 0) < 2 * BLOCK
    ones_hat = (key_lo == out_lo).astype(BF16)

    c_parts, pending_zero = [], None
    for blk in range(q_tile // BLOCK):
        start = pl.multiple_of(tile_start + blk * BLOCK, BLOCK)
        prev = pl.multiple_of(jnp.maximum(start - BLOCK, 0), BLOCK)
        valid = band & (start + kj - BLOCK >= 0) if blk == 0 else band
        rows = pl.ds(blk * BLOCK, BLOCK)

        def band_of(ref):
            return jnp.concatenate([ref[pl.ds(prev, BLOCK), :], ref[pl.ds(start, BLOCK), :]], axis=0)

        kf, vf = band_of(k_ref), band_of(v_ref)
        kr, vr = pltpu.roll(kf, HEAD_DIM, axis=1), pltpu.roll(vf, HEAD_DIM, axis=1)
        zero = jnp.zeros_like(kf)
        for h in range(N_KV_HEADS):
            k_a, k_b = (kf, kr) if h == 0 else (kr, kf)
            v_a, v_b = (vf, vr) if h == 0 else (vr, vf)
            k_top = jnp.where(lo, k_a, zero)
            if pending_zero is not None:
                z = jnp.concatenate([pending_zero, pending_zero], axis=0).astype(BF16)
                k_top = jnp.concatenate([k_top[:2 * SUBLANES] + z, k_top[2 * SUBLANES:]], axis=0)
                pending_zero = None
            k_hat = jnp.concatenate([k_top, jnp.where(lo, zero, k_b)], axis=0)
            v_hat = jnp.concatenate([jnp.where(lo, v_a, zero), jnp.where(lo, zero, v_b)], axis=0)
            q_h = jnp.concatenate(
                [q_ref[rows, pl.ds((h * tiles_per_kv + c) * LANES, LANES)] for c in range(tiles_per_kv)], axis=0)
            s = lax.dot_general(q_h, k_hat, (((1,), (1,)), ((), ())), preferred_element_type=F32)
            probs, sink_terms = [], []
            for half in range(2):
                sh = jnp.where(valid, s[:, half * 2 * BLOCK:(half + 1) * 2 * BLOCK], NEG_INF)
                sink = jnp.concatenate(
                    [jnp.full((BLOCK, LANES), sinks_ref[2 * (h * tiles_per_kv + c) + half] * LOG2E, F32)
                     for c in range(tiles_per_kv)], axis=0)
                m = jnp.max(jnp.maximum(jnp.maximum(sh[:, :LANES], sh[:, LANES:]), sink), axis=-1, keepdims=True)
                probs.append(jnp.exp2(sh - m).astype(BF16))
                sink_terms.append(jnp.exp2(sink - m))
            o_aug = _dot(jnp.concatenate(probs, axis=1), jnp.concatenate([v_hat, ones_hat], axis=1))
            denom = o_aug[:, LANES:] + jnp.where(out_lo, sink_terms[0], sink_terms[1])
            o2 = (o_aug[:, :LANES] / denom).astype(BF16)
            for c in range(tiles_per_kv):
                attn_ref[rows, pl.ds((h * tiles_per_kv + c) * LANES, LANES)] = o2[c * BLOCK:(c + 1) * BLOCK]
            if len(c_parts) * MERGE_CHUNK < D_MODEL:
                col = len(c_parts) * MERGE_CHUNK
                c_parts.append(_dot(c_ref[...], wc_s[:, col:col + MERGE_CHUNK]))
                pending_zero = _zero_after([c_parts[-1][-SUBLANES:, -LANES:]])

    while len(c_parts) * MERGE_CHUNK < D_MODEL:
        col = len(c_parts) * MERGE_CHUNK
        c_parts.append(_dot(c_ref[...], wc_s[:, col:col + MERGE_CHUNK]))
    c = jnp.concatenate(c_parts, axis=1)
    a = _dot(attn_ref[...], wa_s[...])
    merged = (ga_ref[...].astype(F32) * a + gb_ref[...].astype(F32) * c).astype(BF16)
    o_ref[...] = x_ref[...] + _dot(merged, wo_s[...])


def _cast_rows(rows, steps):
    for n in range(steps, 0, -1):
        if rows % n == 0 and (rows // n) % (2 * SUBLANES) == 0:
            return rows // n
    raise ValueError(f"no row split of {rows} over {steps} steps")


def _attention_merge(x, q, k, v, sinks, c_act, ga, gb, wa, wc, wo, weights, batch, seq, q_tile):
    t = q.shape[0]
    tiles = seq // q_tile
    steps = batch * tiles
    cast_specs = []
    for w in weights:
        rp = _cast_rows(w.shape[0], steps)
        last = w.shape[0] // rp - 1
        cast_specs.append(pl.BlockSpec((rp, w.shape[1]),
                                       lambda b, i, s, last=last: (jnp.minimum(b * tiles + i, last), 0)))
    row = pl.BlockSpec((q_tile, D_MODEL), lambda b, i, s: (b * tiles + i, 0))
    kv = pl.BlockSpec((seq, KV_W), lambda b, i, s: (b, 0))
    mat = _const_spec((D_MODEL, D_MODEL))
    return pl.pallas_call(
        functools.partial(_attn_merge_kernel, q_tile=q_tile, n_casts=len(weights)),
        grid_spec=pltpu.PrefetchScalarGridSpec(
            num_scalar_prefetch=1,
            grid=(batch, tiles),
            in_specs=[row, kv, kv, row, row, row, row, mat, mat, mat] + cast_specs,
            out_specs=[row] + cast_specs,
            scratch_shapes=[pltpu.VMEM((q_tile, ATTN_W), BF16)] + [pltpu.VMEM((D_MODEL, D_MODEL), BF16)] * 3,
        ),
        out_shape=[jax.ShapeDtypeStruct((t, D_MODEL), F32)] + [jax.ShapeDtypeStruct(w.shape, BF16) for w in weights],
        compiler_params=_params(2),
        name="swa_merge",
    )(sinks, q, k, v, x, c_act, ga, gb, wa, wc, wo, *weights)


def _rope_tables(seq):
    half = ROPE_DIM // 2
    inv_freq = ROPE_THETA ** (-jnp.arange(0, ROPE_DIM, 2, dtype=F32) / ROPE_DIM)
    ang = jnp.arange(seq, dtype=F32)[:, None] * inv_freq[None, :]
    cos, sin = jnp.cos(ang), jnp.sin(ang)
    d = np.arange(LANES) % HEAD_DIM
    cos_l, sin_l = jnp.ones((seq, LANES), F32), jnp.zeros((seq, LANES), F32)
    for j in range(half):
        pick = (d < ROPE_DIM) & (d % half == j)
        cos_l = jnp.where(pick, cos[:, j:j + 1], cos_l)
        sin_l = jnp.where(pick, sin[:, j:j + 1], sin_l)
    return cos_l, jnp.where(d < half, -sin_l, 0.0), jnp.where(d >= half, sin_l, 0.0)


def _block_ones(width):
    head = np.arange(width) // HEAD_DIM
    return jnp.asarray(head[:, None] == head[None, :], BF16)


def kernel(x, p, ffn1_norm, ffn1_wgu, ffn1_wdown, mix_norm, w_in, q_norm, k_norm, sinks, attn_proj, dw_w, dw_b, conv_ln_g, conv_ln_b, conv_proj, w_out, ffn2_norm, ffn2_wgu, ffn2_wdown, ple_proj, ple_norm, ple_gate_norm, ple_gate_w):
    batch, seq, _ = x.shape
    depth = ffn1_norm.shape[0]
    t = batch * seq
    xt = x.reshape(t, D_MODEL)
    c_tab, s_fwd, s_bwd = _rope_tables(seq)
    ones_q, ones_k = _block_ones(2 * LANES), _block_ones(KV_W)
    vec = lambda a: a.reshape(1, -1)
    bf = lambda a: a.astype(BF16)
    for i in range(depth):
        xt, q, k, v, ga, gb, c_act = _front(
            xt, vec(ffn1_norm[i]), bf(ffn1_wgu[i]), bf(ffn1_wdown[i]), vec(mix_norm[i]), bf(w_in[i]),
            ones_q, ones_k, vec(jnp.tile(q_norm[i], N_Q_HEADS)), vec(jnp.tile(k_norm[i], N_KV_HEADS)),
            c_tab, s_fwd, s_bwd, dw_w[i], dw_b[i], vec(conv_ln_g[i]), vec(conv_ln_b[i]), FRONT_TILE, seq)
        later = [ffn2_wgu[i], ffn2_wdown[i], ple_proj[i], ple_gate_w[i]]
        xt, wgu2, wd2, wple, wgate = _attention_merge(xt, q, k, v, sinks[i], c_act, ga, gb, attn_proj[i],
                                                      conv_proj[i], w_out[i], later, batch, seq, TILE)
        xt = _ffn_ple(xt, p[i].reshape(t, PLE_DIM), vec(ffn2_norm[i]), wgu2, wd2,
                      wple, vec(ple_norm[i]), vec(ple_gate_norm[i]), wgate, TILE)
    return xt.reshape(batch, seq, D_MODEL)
```

```python
import functools

import jax
import jax.numpy as jnp
import numpy as np
from jax import lax
from jax.experimental import pallas as pl
from jax.experimental.pallas import tpu as pltpu

D_MODEL = 1024
N_Q_HEADS = 16
N_KV_HEADS = 2
HEAD_DIM = 64
WINDOW = 128
BLOCK = 128
ROPE_THETA = 500000.0
ROPE_DIM = HEAD_DIM // 4
ATTN_W = N_Q_HEADS * HEAD_DIM
KV_W = N_KV_HEADS * HEAD_DIM
CONV_WIDTH = 31
D_FF = 2816
PLE_DIM = 256
EPS = 1e-6
NEG_INF = -1e30
IN_K = ATTN_W
IN_C = IN_K + 2 * KV_W
IN_G = IN_C + 2 * D_MODEL
LOG2E = 1.4426950408889634

LANES = 128
SUBLANES = 8
assert D_MODEL == SUBLANES * LANES
CONV_HALO = 32
CONV_STEPS = 8
LN_ROWS = 32
MERGE_CHUNK = 256
FF_CHUNK = 256
assert D_FF % FF_CHUNK == 0
VMEM_LIMIT = 60 * 1024 * 1024
FRONT_TILE = 512
TILE = 512

BF16 = jnp.bfloat16
F32 = jnp.float32


def _dot(a, b):
    return jnp.dot(a, b, preferred_element_type=F32)


def _rmsnorm(x, g):
    return x * lax.rsqrt(jnp.mean(x * x, axis=-1, keepdims=True) + EPS) * g


def _const_spec(shape):
    return pl.BlockSpec(shape, lambda *_: (0,) * len(shape), pipeline_mode=pl.Buffered(1))


def _params(n_axes):
    return pltpu.CompilerParams(dimension_semantics=("arbitrary",) * n_axes, vmem_limit_bytes=VMEM_LIMIT)


def _swiglu_update(x, g_ref, wgu_ref, wd_ref):
    h = _rmsnorm(x, g_ref[...]).astype(BF16)
    g = _dot(h, wgu_ref[:, :D_FF])
    u = _dot(h, wgu_ref[:, D_FF:])
    a = (g * jax.nn.sigmoid(g) * u).astype(BF16)
    return x + 0.5 * _dot(a, wd_ref[...])


def _zero_after(values):
    tiles = [v[r:r + SUBLANES, c:c + LANES] for v in values
             for r in range(0, v.shape[0], SUBLANES) for c in range(0, v.shape[1], LANES)]
    half_word = jnp.uint32(16)
    bits = functools.reduce(jnp.bitwise_or, [pltpu.bitcast(t, jnp.uint32) for t in tiles])
    return lax.shift_right_logical(lax.shift_right_logical(bits, half_word), half_word).astype(F32)


class _Weaver:
    def __init__(self, lhs_ref, piece):
        self.lhs_ref, self.piece, self.zero = lhs_ref, piece, None

    def after_matmul(self, n=1):
        tile = (pl.ds(0, 2 * SUBLANES), pl.ds(0, LANES))
        if self.zero is not None:
            z = jnp.concatenate([self.zero, self.zero], axis=0)
            self.lhs_ref[tile] = (self.lhs_ref[tile].astype(F32) + z).astype(BF16)
        results = []
        for _ in range(n):
            results.extend(self.piece() or [])
        self.zero = _zero_after(results) if results else None


def _swiglu_update_chunked(x, g_ref, wgu_ref, wd_ref, weaver, n_pieces):
    h_ref = weaver.lhs_ref
    h_ref[...] = _rmsnorm(x, g_ref[...]).astype(BF16)
    y = None
    for c in range(0, D_FF, FF_CHUNK):
        g = _dot(h_ref[...], wgu_ref[:, c:c + FF_CHUNK])
        weaver.after_matmul(n_pieces)
        u = _dot(h_ref[...], wgu_ref[:, D_FF + c:D_FF + c + FF_CHUNK])
        weaver.after_matmul(n_pieces)
        a = (g * jax.nn.sigmoid(g) * u).astype(BF16)
        d = _dot(a, wd_ref[c:c + FF_CHUNK, :])
        weaver.after_matmul(n_pieces)
        y = d if y is None else y + d
    return x + 0.5 * y


def _ffn_ple_kernel(x_ref, p_ref, g_ref, wgu_ref, wd_ref, pw_ref, pn_ref, gn_ref, gw_ref, o_ref):
    x = _swiglu_update(x_ref[...], g_ref, wgu_ref, wd_ref)
    e = _rmsnorm(_dot(p_ref[...].astype(BF16), pw_ref[...]), pn_ref[...])
    gate = jax.nn.sigmoid(_dot(_rmsnorm(x, gn_ref[...]).astype(BF16), gw_ref[...]))
    o_ref[...] = x + gate * e


def _ffn_ple(x, p, norm, wgu, wd, ple_w, ple_norm, gate_norm, gate_w, tm):
    t = x.shape[0]
    row = pl.BlockSpec((tm, D_MODEL), lambda i: (i, 0))
    vec = _const_spec((1, D_MODEL))
    return pl.pallas_call(
        _ffn_ple_kernel,
        grid=(t // tm,),
        in_specs=[row, pl.BlockSpec((tm, PLE_DIM), lambda i: (i, 0)), vec,
                  _const_spec((D_MODEL, 2 * D_FF)), _const_spec((D_FF, D_MODEL)),
                  _const_spec((PLE_DIM, D_MODEL)), vec, vec, _const_spec((D_MODEL, D_MODEL))],
        out_specs=row,
        out_shape=jax.ShapeDtypeStruct((t, D_MODEL), F32),
        compiler_params=_params(1),
        name="ffn2_ple",
    )(x, p, norm, wgu, wd, ple_w, ple_norm, gate_norm, gate_w)


def _head_norm_rope(z, ones_blk, gain, c_tab, s_fwd, s_bwd):
    w = z.shape[1]
    blk = ones_blk.shape[0]
    sq = (z * z).astype(BF16)
    ss = jnp.concatenate([_dot(sq[:, i:i + blk], ones_blk) for i in range(0, w, blk)], axis=1)
    t = z * lax.rsqrt(ss * (1.0 / HEAD_DIM) + EPS) * gain
    half = ROPE_DIM // 2
    up = pltpu.roll(t, w - half, axis=1)
    dn = pltpu.roll(t, half, axis=1)
    reps = w // LANES
    tile = lambda a: jnp.concatenate([a] * reps, axis=1) if reps > 1 else a
    return t * tile(c_tab) + up * tile(s_fwd) + dn * tile(s_bwd)


def _conv_ln_silu(tmaj_ref, y_ref, dw_ref, db_ref, lg_ref, lb_ref, out_ref, tm):
    base = CONV_HALO - (CONV_WIDTH - 1)

    def conv_piece(t0):
        acc = [None] * CONV_STEPS
        for j in range(CONV_WIDTH):
            w_j = dw_ref[pl.ds(j * SUBLANES, SUBLANES), :]
            for r in range(CONV_STEPS):
                v = tmaj_ref[pl.ds((t0 + r + base + j) * SUBLANES, SUBLANES), :] * w_j
                acc[r] = v if j == 0 else acc[r] + v
        bias = db_ref[...]
        acc = [a + bias for a in acc]
        for r in range(CONV_STEPS):
            y_ref[pl.ds((t0 + r) * SUBLANES, SUBLANES), :] = acc[r]
        return acc

    def ln_piece(t0):
        y = jnp.concatenate([y_ref[pl.ds(t0 * SUBLANES + c, LN_ROWS, stride=SUBLANES), :]
                             for c in range(SUBLANES)], axis=1)
        yc = y - jnp.mean(y, axis=-1, keepdims=True)
        var = jnp.mean(yc * yc, axis=-1, keepdims=True)
        ln = yc * lax.rsqrt(var + EPS) * lg_ref[...] + lb_ref[...]
        act = ln * jax.nn.sigmoid(ln)
        out_ref[pl.ds(t0, LN_ROWS), :] = act.astype(BF16)
        return [act]

    pending = [functools.partial(conv_piece, t0) for t0 in range(0, tm, CONV_STEPS)]
    pending += [functools.partial(ln_piece, t0) for t0 in range(0, tm, LN_ROWS)]

    def piece():
        return pending.pop(0)() if pending else None

    def finish():
        while pending:
            piece()

    return piece, finish


def _front_kernel(x_ref, n1_ref, wgu_ref, wd_ref, nm_ref, win_ref,
                  oq_ref, ok_ref, gq_ref, gk_ref, c_ref, sf_ref, sb_ref, dw_ref, db_ref, lg_ref, lb_ref,
                  x1_out, q_out, k_out, v_out, ga_out, gb_out, cact_out, tmaj_ref, y_ref, h_ref, *, tm, tiles_per_seq):
    step = pl.program_id(0)
    halo_rows = CONV_HALO * SUBLANES

    @pl.when(step == 0)
    def _():
        tmaj_ref[...] = jnp.zeros(tmaj_ref.shape, F32)

    conv = functools.partial(_conv_ln_silu, tmaj_ref, y_ref, dw_ref, db_ref, lg_ref, lb_ref, cact_out, tm)
    last = pl.num_programs(0) - 1

    @pl.when(step == last)
    def _():
        conv()[1]()

    @pl.when(step < last)
    def _():
        conv_piece, conv_finish = conv()
        weaver = _Weaver(h_ref, conv_piece)
        ffn_matmuls = 3 * (D_FF // FF_CHUNK)
        x1 = _swiglu_update_chunked(x_ref[...], n1_ref, wgu_ref, wd_ref, weaver,
                                    pl.cdiv(tm // CONV_STEPS, ffn_matmuls))
        k = tm // (8 * LN_ROWS)
        x1_out[...] = x1
        h_ref[...] = _rmsnorm(x1, nm_ref[...]).astype(BF16)
        c_tab, s_fwd, s_bwd = c_ref[...], sf_ref[...], sb_ref[...]
        zq = _dot(h_ref[...], win_ref[:, :IN_K])
        weaver.after_matmul(k)
        scale = HEAD_DIM ** -0.5 * LOG2E
        q_out[...] = (_head_norm_rope(zq, oq_ref[...], gq_ref[...], c_tab, s_fwd, s_bwd) * scale).astype(BF16)
        zkv = _dot(h_ref[...], win_ref[:, IN_K:IN_C])
        weaver.after_matmul(k)
        k_out[...] = _head_norm_rope(zkv[:, :KV_W], ok_ref[...], gk_ref[...], c_tab, s_fwd, s_bwd).astype(BF16)
        v_out[...] = zkv[:, KV_W:].astype(BF16)
        za = _dot(h_ref[...], win_ref[:, IN_G:IN_G + D_MODEL])
        weaver.after_matmul(2 * k)
        ga_out[...] = jax.nn.sigmoid(za).astype(BF16)
        zb = _dot(h_ref[...], win_ref[:, IN_G + D_MODEL:])
        weaver.after_matmul(2 * k)
        gb_out[...] = jax.nn.sigmoid(zb).astype(BF16)
        u_lin = _dot(h_ref[...], win_ref[:, IN_C:IN_C + D_MODEL])
        weaver.after_matmul(2 * k)
        conv_finish()
        u = u_lin * jax.nn.sigmoid(_dot(h_ref[...], win_ref[:, IN_C + D_MODEL:IN_G]))

        tail = tmaj_ref[pl.ds(tm * SUBLANES, halo_rows), :]
        tmaj_ref[pl.ds(0, halo_rows), :] = jnp.where(step % tiles_per_seq == 0, jnp.zeros_like(tail), tail)
        for c in range(SUBLANES):
            tmaj_ref[pl.ds(halo_rows + c, tm, stride=SUBLANES), :] = u[:, c * LANES:(c + 1) * LANES]


def _front(x, n1, wgu, wd, nm, w_in, ones_q, ones_k, gq, gk, c_tab, s_fwd, s_bwd,
           dw_w, dw_b, ln_g, ln_b, tm, seq):
    t = x.shape[0]
    n_tiles = t // tm
    tiles_per_seq = seq // tm
    cur = lambda s: jnp.minimum(s, n_tiles - 1)
    row = lambda w: pl.BlockSpec((tm, w), lambda s: (cur(s), 0))
    lag = pl.BlockSpec((tm, D_MODEL), lambda s: (jnp.maximum(s - 1, 0), 0))
    tab = pl.BlockSpec((tm, LANES), lambda s: (cur(s) % tiles_per_seq, 0))
    vec = _const_spec((1, D_MODEL))
    out = lambda w, dt=BF16: jax.ShapeDtypeStruct((t, w), dt)
    dw_tmaj = dw_w.reshape(CONV_WIDTH * SUBLANES, LANES)
    db_tmaj = dw_b.reshape(SUBLANES, LANES)
    return pl.pallas_call(
        functools.partial(_front_kernel, tm=tm, tiles_per_seq=tiles_per_seq),
        grid=(n_tiles + 1,),
        in_specs=[row(D_MODEL), vec, _const_spec(wgu.shape), _const_spec(wd.shape), vec, _const_spec(w_in.shape),
                  _const_spec(ones_q.shape), _const_spec(ones_k.shape),
                  _const_spec((1, ATTN_W)), _const_spec((1, KV_W)), tab, tab, tab,
                  _const_spec(dw_tmaj.shape), _const_spec(db_tmaj.shape), vec, vec],
        out_specs=[row(D_MODEL), row(ATTN_W), row(KV_W), row(KV_W), row(D_MODEL), row(D_MODEL), lag],
        out_shape=[out(D_MODEL, F32), out(ATTN_W), out(KV_W), out(KV_W), out(D_MODEL), out(D_MODEL),
                   out(D_MODEL)],
        scratch_shapes=[pltpu.VMEM(((CONV_HALO + tm) * SUBLANES, LANES), F32),
                        pltpu.VMEM((tm * SUBLANES, LANES), F32),
                        pltpu.VMEM((tm, D_MODEL), BF16)],
        compiler_params=_params(1),
        name="ffn1_proj_conv",
    )(x, n1, wgu, wd, nm, w_in, ones_q, ones_k, gq, gk, c_tab, s_fwd, s_bwd,
      dw_tmaj, db_tmaj, ln_g, ln_b)


def _attn_merge_kernel(sinks_ref, q_ref, k_ref, v_ref, x_ref, c_ref, ga_ref, gb_ref, wa_ref, wc_ref, wo_ref,
                       *rest, q_tile, n_casts):
    o_ref = rest[n_casts]
    attn_ref, wa_s, wc_s, wo_s = rest[2 * n_casts + 1:]
    for src, dst in zip(rest[:n_casts], rest[n_casts + 1:2 * n_casts + 1]):
        dst[...] = src[...].astype(BF16)

    @pl.when((pl.program_id(0) == 0) & (pl.program_id(1) == 0))
    def _():
        for w_ref, w_s in ((wa_ref, wa_s), (wc_ref, wc_s), (wo_ref, wo_s)):
            w_s[...] = w_ref[...].astype(BF16)

    tile_start = pl.program_id(1) * q_tile
    lane = lax.broadcasted_iota(jnp.int32, (2 * BLOCK, LANES), 1)
    lo = lane < HEAD_DIM
    tiles_per_kv = ATTN_W // LANES // N_KV_HEADS
    qi = lax.broadcasted_iota(jnp.int32, (tiles_per_kv * BLOCK, 2 * BLOCK), 0) % BLOCK
    kj = lax.broadcasted_iota(jnp.int32, (tiles_per_kv * BLOCK, 2 * BLOCK), 1)
    delta = qi + BLOCK - kj
    band = (delta >= 0) & (delta < WINDOW)
    out_lo = lax.broadcasted_iota(jnp.int32, (tiles_per_kv * BLOCK, LANES), 1) < HEAD_DIM
    key_lo = lax.broadcasted_iota(jnp.int32, (2 * 2 * BLOCK, LANES), 0) < 2 * BLOCK
    ones_hat = (key_lo == out_lo).astype(BF16)

    c_parts, pending_zero = [], None
    for blk in range(q_tile // BLOCK):
        start = pl.multiple_of(tile_start + blk * BLOCK, BLOCK)
        prev = pl.multiple_of(jnp.maximum(start - BLOCK, 0), BLOCK)
        valid = band & (start + kj - BLOCK >= 0) if blk == 0 else band
        rows = pl.ds(blk * BLOCK, BLOCK)

        def band_of(ref):
            return jnp.concatenate([ref[pl.ds(prev, BLOCK), :], ref[pl.ds(start, BLOCK), :]], axis=0)

        kf, vf = band_of(k_ref), band_of(v_ref)
        kr, vr = pltpu.roll(kf, HEAD_DIM, axis=1), pltpu.roll(vf, HEAD_DIM, axis=1)
        zero = jnp.zeros_like(kf)
        for h in range(N_KV_HEADS):
            k_a, k_b = (kf, kr) if h == 0 else (kr, kf)
            v_a, v_b = (vf, vr) if h == 0 else (vr, vf)
            k_top = jnp.where(lo, k_a, zero)
            if pending_zero is not None:
                z = jnp.concatenate([pending_zero, pending_zero], axis=0).astype(BF16)
                k_top = jnp.concatenate([k_top[:2 * SUBLANES] + z, k_top[2 * SUBLANES:]], axis=0)
                pending_zero = None
            k_hat = jnp.concatenate([k_top, jnp.where(lo, zero, k_b)], axis=0)
            v_hat = jnp.concatenate([jnp.where(lo, v_a, zero), jnp.where(lo, zero, v_b)], axis=0)
            q_h = jnp.concatenate(
                [q_ref[rows, pl.ds((h * tiles_per_kv + c) * LANES, LANES)] for c in range(tiles_per_kv)], axis=0)
            s = lax.dot_general(q_h, k_hat, (((1,), (1,)), ((), ())), preferred_element_type=F32)
            probs, sink_terms = [], []
            for half in range(2):
                sh = jnp.where(valid, s[:, half * 2 * BLOCK:(half + 1) * 2 * BLOCK], NEG_INF)
                sink = jnp.concatenate(
                    [jnp.full((BLOCK, LANES), sinks_ref[2 * (h * tiles_per_kv + c) + half] * LOG2E, F32)
                     for c in range(tiles_per_kv)], axis=0)
                m = jnp.max(jnp.maximum(jnp.maximum(sh[:, :LANES], sh[:, LANES:]), sink), axis=-1, keepdims=True)
                probs.append(jnp.exp2(sh - m).astype(BF16))
                sink_terms.append(jnp.exp2(sink - m))
            o_aug = _dot(jnp.concatenate(probs, axis=1), jnp.concatenate([v_hat, ones_hat], axis=1))
            denom = o_aug[:, LANES:] + jnp.where(out_lo, sink_terms[0], sink_terms[1])
            o2 = (o_aug[:, :LANES] / denom).astype(BF16)
            for c in range(tiles_per_kv):
                attn_ref[rows, pl.ds((h * tiles_per_kv + c) * LANES, LANES)] = o2[c * BLOCK:(c + 1) * BLOCK]
            if len(c_parts) * MERGE_CHUNK < D_MODEL:
                col = len(c_parts) * MERGE_CHUNK
                c_parts.append(_dot(c_ref[...], wc_s[:, col:col + MERGE_CHUNK]))
                pending_zero = _zero_after([c_parts[-1][-SUBLANES:, -LANES:]])

    while len(c_parts) * MERGE_CHUNK < D_MODEL:
        col = len(c_parts) * MERGE_CHUNK
        c_parts.append(_dot(c_ref[...], wc_s[:, col:col + MERGE_CHUNK]))
    c = jnp.concatenate(c_parts, axis=1)
    a = _dot(attn_ref[...], wa_s[...])
    merged = (ga_ref[...].astype(F32) * a + gb_ref[...].astype(F32) * c).astype(BF16)
    o_ref[...] = x_ref[...] + _dot(merged, wo_s[...])


def _cast_rows(rows, steps):
    for n in range(steps, 0, -1):
        if rows % n == 0 and (rows // n) % (2 * SUBLANES) == 0:
            return rows // n
    raise ValueError(f"no row split of {rows} over {steps} steps")


def _attention_merge(x, q, k, v, sinks, c_act, ga, gb, wa, wc, wo, weights, batch, seq, q_tile):
    t = q.shape[0]
    tiles = seq // q_tile
    steps = batch * tiles
    cast_specs = []
    for w in weights:
        rp = _cast_rows(w.shape[0], steps)
        last = w.shape[0] // rp - 1
        cast_specs.append(pl.BlockSpec((rp, w.shape[1]),
                                       lambda b, i, s, last=last: (jnp.minimum(b * tiles + i, last), 0)))
    row = pl.BlockSpec((q_tile, D_MODEL), lambda b, i, s: (b * tiles + i, 0))
    kv = pl.BlockSpec((seq, KV_W), lambda b, i, s: (b, 0))
    mat = _const_spec((D_MODEL, D_MODEL))
    return pl.pallas_call(
        functools.partial(_attn_merge_kernel, q_tile=q_tile, n_casts=len(weights)),
        grid_spec=pltpu.PrefetchScalarGridSpec(
            num_scalar_prefetch=1,
            grid=(batch, tiles),
            in_specs=[row, kv, kv, row, row, row, row, mat, mat, mat] + cast_specs,
            out_specs=[row] + cast_specs,
            scratch_shapes=[pltpu.VMEM((q_tile, ATTN_W), BF16)] + [pltpu.VMEM((D_MODEL, D_MODEL), BF16)] * 3,
        ),
        out_shape=[jax.ShapeDtypeStruct((t, D_MODEL), F32)] + [jax.ShapeDtypeStruct(w.shape, BF16) for w in weights],
        compiler_params=_params(2),
        name="swa_merge",
    )(sinks, q, k, v, x, c_act, ga, gb, wa, wc, wo, *weights)


def _rope_tables(seq):
    half = ROPE_DIM // 2
    inv_freq = ROPE_THETA ** (-jnp.arange(0, ROPE_DIM, 2, dtype=F32) / ROPE_DIM)
    ang = jnp.arange(seq, dtype=F32)[:, None] * inv_freq[None, :]
    cos, sin = jnp.cos(ang), jnp.sin(ang)
    d = np.arange(LANES) % HEAD_DIM
    cos_l, sin_l = jnp.ones((seq, LANES), F32), jnp.zeros((seq, LANES), F32)
    for j in range(half):
        pick = (d < ROPE_DIM) & (d % half == j)
        cos_l = jnp.where(pick, cos[:, j:j + 1], cos_l)
        sin_l = jnp.where(pick, sin[:, j:j + 1], sin_l)
    return cos_l, jnp.where(d < half, -sin_l, 0.0), jnp.where(d >= half, sin_l, 0.0)


def _block_ones(width):
    head = np.arange(width) // HEAD_DIM
    return jnp.asarray(head[:, None] == head[None, :], BF16)


def kernel(x, p, ffn1_norm, ffn1_wgu, ffn1_wdown, mix_norm, w_in, q_norm, k_norm, sinks, attn_proj, dw_w, dw_b, conv_ln_g, conv_ln_b, conv_proj, w_out, ffn2_norm, ffn2_wgu, ffn2_wdown, ple_proj, ple_norm, ple_gate_norm, ple_gate_w):
    batch, seq, _ = x.shape
    depth = ffn1_norm.shape[0]
    t = batch * seq
    xt = x.reshape(t, D_MODEL)
    c_tab, s_fwd, s_bwd = _rope_tables(seq)
    ones_q, ones_k = _block_ones(2 * LANES), _block_ones(KV_W)
    vec = lambda a: a.reshape(1, -1)
    bf = lambda a: a.astype(BF16)
    for i in range(depth):
        xt, q, k, v, ga, gb, c_act = _front(
            xt, vec(ffn1_norm[i]), bf(ffn1_wgu[i]), bf(ffn1_wdown[i]), vec(mix_norm[i]), bf(w_in[i]),
            ones_q, ones_k, vec(jnp.tile(q_norm[i], N_Q_HEADS)), vec(jnp.tile(k_norm[i], N_KV_HEADS)),
            c_tab, s_fwd, s_bwd, dw_w[i], dw_b[i], vec(conv_ln_g[i]), vec(conv_ln_b[i]), FRONT_TILE, seq)
        later = [ffn2_wgu[i], ffn2_wdown[i], ple_proj[i], ple_gate_w[i]]
        xt, wgu2, wd2, wple, wgate = _attention_merge(xt, q, k, v, sinks[i], c_act, ga, gb, attn_proj[i],
                                                      conv_proj[i], w_out[i], later, batch, seq, TILE)
        xt = _ffn_ple(xt, p[i].reshape(t, PLE_DIM), vec(ffn2_norm[i]), wgu2, wd2,
                      wple, vec(ple_norm[i]), vec(ple_gate_norm[i]), wgate, TILE)
    return xt.reshape(batch, seq, D_MODEL)
```

```python
import functools

import jax
import jax.numpy as jnp
import numpy as np
from jax import lax
from jax.experimental import pallas as pl
from jax.experimental.pallas import tpu as pltpu

D_MODEL = 1024
N_Q_HEADS = 16
N_KV_HEADS = 2
HEAD_DIM = 64
WINDOW = 128
BLOCK = 128
ROPE_THETA = 500000.0
ROPE_DIM = HEAD_DIM // 4
ATTN_W = N_Q_HEADS * HEAD_DIM
KV_W = N_KV_HEADS * HEAD_DIM
CONV_WIDTH = 31
D_FF = 2816
PLE_DIM = 256
EPS = 1e-6
NEG_INF = -1e30
IN_K = ATTN_W
IN_C = IN_K + 2 * KV_W
IN_G = IN_C + 2 * D_MODEL
LOG2E = 1.4426950408889634

LANES = 128
SUBLANES = 8
assert D_MODEL == SUBLANES * LANES
CONV_HALO = 32
CONV_STEPS = 4
LN_ROWS = 32
MERGE_CHUNK = 256
FF_CHUNK = 256
assert D_FF % FF_CHUNK == 0
VMEM_LIMIT = 60 * 1024 * 1024
FRONT_TILE = 512
TILE = 512

BF16 = jnp.bfloat16
F32 = jnp.float32


def _dot(a, b):
    return jnp.dot(a, b, preferred_element_type=F32)


def _rmsnorm(x, g):
    return x * lax.rsqrt(jnp.mean(x * x, axis=-1, keepdims=True) + EPS) * g


def _const_spec(shape):
    return pl.BlockSpec(shape, lambda *_: (0,) * len(shape), pipeline_mode=pl.Buffered(1))


def _params(n_axes):
    return pltpu.CompilerParams(dimension_semantics=("arbitrary",) * n_axes, vmem_limit_bytes=VMEM_LIMIT)


def _swiglu_update(x, g_ref, wgu_ref, wd_ref):
    h = _rmsnorm(x, g_ref[...]).astype(BF16)
    g = _dot(h, wgu_ref[:, :D_FF])
    u = _dot(h, wgu_ref[:, D_FF:])
    a = (g * jax.nn.sigmoid(g) * u).astype(BF16)
    return x + 0.5 * _dot(a, wd_ref[...])


def _zero_after(values):
    tiles = [v[r:r + SUBLANES, c:c + LANES] for v in values
             for r in range(0, v.shape[0], SUBLANES) for c in range(0, v.shape[1], LANES)]
    half_word = jnp.uint32(16)
    bits = functools.reduce(jnp.bitwise_or, [pltpu.bitcast(t, jnp.uint32) for t in tiles])
    return lax.shift_right_logical(lax.shift_right_logical(bits, half_word), half_word).astype(F32)


class _Weaver:
    def __init__(self, lhs_ref, piece):
        self.lhs_ref, self.piece, self.zero = lhs_ref, piece, None

    def after_matmul(self, n=1):
        tile = (pl.ds(0, 2 * SUBLANES), pl.ds(0, LANES))
        if self.zero is not None:
            z = jnp.concatenate([self.zero, self.zero], axis=0)
            self.lhs_ref[tile] = (self.lhs_ref[tile].astype(F32) + z).astype(BF16)
        results = []
        for _ in range(n):
            results.extend(self.piece() or [])
        self.zero = _zero_after(results) if results else None


def _swiglu_update_chunked(x, g_ref, wgu_ref, wd_ref, weaver, n_pieces):
    h_ref = weaver.lhs_ref
    h_ref[...] = _rmsnorm(x, g_ref[...]).astype(BF16)
    y = None
    for c in range(0, D_FF, FF_CHUNK):
        g = _dot(h_ref[...], wgu_ref[:, c:c + FF_CHUNK])
        weaver.after_matmul(n_pieces)
        u = _dot(h_ref[...], wgu_ref[:, D_FF + c:D_FF + c + FF_CHUNK])
        weaver.after_matmul(n_pieces)
        a = (g * jax.nn.sigmoid(g) * u).astype(BF16)
        d = _dot(a, wd_ref[c:c + FF_CHUNK, :])
        weaver.after_matmul(n_pieces)
        y = d if y is None else y + d
    return x + 0.5 * y


def _ffn_ple_kernel(x_ref, p_ref, g_ref, wgu_ref, wd_ref, pw_ref, pn_ref, gn_ref, gw_ref, o_ref):
    x = _swiglu_update(x_ref[...], g_ref, wgu_ref, wd_ref)
    e = _rmsnorm(_dot(p_ref[...].astype(BF16), pw_ref[...]), pn_ref[...])
    gate = jax.nn.sigmoid(_dot(_rmsnorm(x, gn_ref[...]).astype(BF16), gw_ref[...]))
    o_ref[...] = x + gate * e


def _ffn_ple(x, p, norm, wgu, wd, ple_w, ple_norm, gate_norm, gate_w, tm):
    t = x.shape[0]
    row = pl.BlockSpec((tm, D_MODEL), lambda i: (i, 0))
    vec = _const_spec((1, D_MODEL))
    return pl.pallas_call(
        _ffn_ple_kernel,
        grid=(t // tm,),
        in_specs=[row, pl.BlockSpec((tm, PLE_DIM), lambda i: (i, 0)), vec,
                  _const_spec((D_MODEL, 2 * D_FF)), _const_spec((D_FF, D_MODEL)),
                  _const_spec((PLE_DIM, D_MODEL)), vec, vec, _const_spec((D_MODEL, D_MODEL))],
        out_specs=row,
        out_shape=jax.ShapeDtypeStruct((t, D_MODEL), F32),
        compiler_params=_params(1),
        name="ffn2_ple",
    )(x, p, norm, wgu, wd, ple_w, ple_norm, gate_norm, gate_w)


def _head_norm_rope(z, ones_blk, gain, c_tab, s_fwd, s_bwd):
    w = z.shape[1]
    blk = ones_blk.shape[0]
    sq = (z * z).astype(BF16)
    ss = jnp.concatenate([_dot(sq[:, i:i + blk], ones_blk) for i in range(0, w, blk)], axis=1)
    t = z * lax.rsqrt(ss * (1.0 / HEAD_DIM) + EPS) * gain
    half = ROPE_DIM // 2
    up = pltpu.roll(t, w - half, axis=1)
    dn = pltpu.roll(t, half, axis=1)
    reps = w // LANES
    tile = lambda a: jnp.concatenate([a] * reps, axis=1) if reps > 1 else a
    return t * tile(c_tab) + up * tile(s_fwd) + dn * tile(s_bwd)


def _conv_ln_silu(tmaj_ref, y_ref, dw_ref, db_ref, lg_ref, lb_ref, out_ref, tm):
    base = CONV_HALO - (CONV_WIDTH - 1)

    def conv_piece(t0):
        acc = [None] * CONV_STEPS
        for j in range(CONV_WIDTH):
            w_j = dw_ref[pl.ds(j * SUBLANES, SUBLANES), :]
            for r in range(CONV_STEPS):
                v = tmaj_ref[pl.ds((t0 + r + base + j) * SUBLANES, SUBLANES), :] * w_j
                acc[r] = v if j == 0 else acc[r] + v
        bias = db_ref[...]
        acc = [a + bias for a in acc]
        for r in range(CONV_STEPS):
            y_ref[pl.ds((t0 + r) * SUBLANES, SUBLANES), :] = acc[r]
        return acc

    def ln_piece(t0):
        y = jnp.concatenate([y_ref[pl.ds(t0 * SUBLANES + c, LN_ROWS, stride=SUBLANES), :]
                             for c in range(SUBLANES)], axis=1)
        yc = y - jnp.mean(y, axis=-1, keepdims=True)
        var = jnp.mean(yc * yc, axis=-1, keepdims=True)
        ln = yc * lax.rsqrt(var + EPS) * lg_ref[...] + lb_ref[...]
        act = ln * jax.nn.sigmoid(ln)
        out_ref[pl.ds(t0, LN_ROWS), :] = act.astype(BF16)
        return [act]

    pending = [functools.partial(conv_piece, t0) for t0 in range(0, tm, CONV_STEPS)]
    pending += [functools.partial(ln_piece, t0) for t0 in range(0, tm, LN_ROWS)]

    def piece():
        return pending.pop(0)() if pending else None

    def finish():
        while pending:
            piece()

    return piece, finish


def _front_kernel(x_ref, n1_ref, wgu_ref, wd_ref, nm_ref, win_ref,
                  oq_ref, ok_ref, gq_ref, gk_ref, c_ref, sf_ref, sb_ref, dw_ref, db_ref, lg_ref, lb_ref,
                  x1_out, q_out, k_out, v_out, ga_out, gb_out, cact_out, tmaj_ref, y_ref, h_ref, *, tm, tiles_per_seq):
    step = pl.program_id(0)
    halo_rows = CONV_HALO * SUBLANES

    @pl.when(step == 0)
    def _():
        tmaj_ref[...] = jnp.zeros(tmaj_ref.shape, F32)

    conv = functools.partial(_conv_ln_silu, tmaj_ref, y_ref, dw_ref, db_ref, lg_ref, lb_ref, cact_out, tm)
    last = pl.num_programs(0) - 1

    @pl.when(step == last)
    def _():
        conv()[1]()

    @pl.when(step < last)
    def _():
        conv_piece, conv_finish = conv()
        weaver = _Weaver(h_ref, conv_piece)
        ffn_matmuls = 3 * (D_FF // FF_CHUNK)
        x1 = _swiglu_update_chunked(x_ref[...], n1_ref, wgu_ref, wd_ref, weaver,
                                    pl.cdiv(tm // CONV_STEPS, ffn_matmuls))
        k = tm // (8 * LN_ROWS)
        x1_out[...] = x1
        h_ref[...] = _rmsnorm(x1, nm_ref[...]).astype(BF16)
        c_tab, s_fwd, s_bwd = c_ref[...], sf_ref[...], sb_ref[...]
        zq = _dot(h_ref[...], win_ref[:, :IN_K])
        weaver.after_matmul(k)
        scale = HEAD_DIM ** -0.5 * LOG2E
        q_out[...] = (_head_norm_rope(zq, oq_ref[...], gq_ref[...], c_tab, s_fwd, s_bwd) * scale).astype(BF16)
        zkv = _dot(h_ref[...], win_ref[:, IN_K:IN_C])
        weaver.after_matmul(k)
        k_out[...] = _head_norm_rope(zkv[:, :KV_W], ok_ref[...], gk_ref[...], c_tab, s_fwd, s_bwd).astype(BF16)
        v_out[...] = zkv[:, KV_W:].astype(BF16)
        za = _dot(h_ref[...], win_ref[:, IN_G:IN_G + D_MODEL])
        weaver.after_matmul(2 * k)
        ga_out[...] = jax.nn.sigmoid(za).astype(BF16)
        zb = _dot(h_ref[...], win_ref[:, IN_G + D_MODEL:])
        weaver.after_matmul(2 * k)
        gb_out[...] = jax.nn.sigmoid(zb).astype(BF16)
        u_lin = _dot(h_ref[...], win_ref[:, IN_C:IN_C + D_MODEL])
        weaver.after_matmul(2 * k)
        conv_finish()
        u = u_lin * jax.nn.sigmoid(_dot(h_ref[...], win_ref[:, IN_C + D_MODEL:IN_G]))

        tail = tmaj_ref[pl.ds(tm * SUBLANES, halo_rows), :]
        tmaj_ref[pl.ds(0, halo_rows), :] = jnp.where(step % tiles_per_seq == 0, jnp.zeros_like(tail), tail)
        for c in range(SUBLANES):
            tmaj_ref[pl.ds(halo_rows + c, tm, stride=SUBLANES), :] = u[:, c * LANES:(c + 1) * LANES]


def _front(x, n1, wgu, wd, nm, w_in, ones_q, ones_k, gq, gk, c_tab, s_fwd, s_bwd,
           dw_w, dw_b, ln_g, ln_b, tm, seq):
    t = x.shape[0]
    n_tiles = t // tm
    tiles_per_seq = seq // tm
    cur = lambda s: jnp.minimum(s, n_tiles - 1)
    row = lambda w: pl.BlockSpec((tm, w), lambda s: (cur(s), 0))
    lag = pl.BlockSpec((tm, D_MODEL), lambda s: (jnp.maximum(s - 1, 0), 0))
    tab = pl.BlockSpec((tm, LANES), lambda s: (cur(s) % tiles_per_seq, 0))
    vec = _const_spec((1, D_MODEL))
    out = lambda w, dt=BF16: jax.ShapeDtypeStruct((t, w), dt)
    dw_tmaj = dw_w.reshape(CONV_WIDTH * SUBLANES, LANES)
    db_tmaj = dw_b.reshape(SUBLANES, LANES)
    return pl.pallas_call(
        functools.partial(_front_kernel, tm=tm, tiles_per_seq=tiles_per_seq),
        grid=(n_tiles + 1,),
        in_specs=[row(D_MODEL), vec, _const_spec(wgu.shape), _const_spec(wd.shape), vec, _const_spec(w_in.shape),
                  _const_spec(ones_q.shape), _const_spec(ones_k.shape),
                  _const_spec((1, ATTN_W)), _const_spec((1, KV_W)), tab, tab, tab,
                  _const_spec(dw_tmaj.shape), _const_spec(db_tmaj.shape), vec, vec],
        out_specs=[row(D_MODEL), row(ATTN_W), row(KV_W), row(KV_W), row(D_MODEL), row(D_MODEL), lag],
        out_shape=[out(D_MODEL, F32), out(ATTN_W), out(KV_W), out(KV_W), out(D_MODEL), out(D_MODEL),
                   out(D_MODEL)],
        scratch_shapes=[pltpu.VMEM(((CONV_HALO + tm) * SUBLANES, LANES), F32),
                        pltpu.VMEM((tm * SUBLANES, LANES), F32),
                        pltpu.VMEM((tm, D_MODEL), BF16)],
        compiler_params=_params(1),
        name="ffn1_proj_conv",
    )(x, n1, wgu, wd, nm, w_in, ones_q, ones_k, gq, gk, c_tab, s_fwd, s_bwd,
      dw_tmaj, db_tmaj, ln_g, ln_b)


def _attn_merge_kernel(sinks_ref, q_ref, k_ref, v_ref, x_ref, c_ref, ga_ref, gb_ref, wa_ref, wc_ref, wo_ref,
                       *rest, q_tile, n_casts):
    o_ref = rest[n_casts]
    attn_ref, wa_s, wc_s, wo_s = rest[2 * n_casts + 1:]
    for src, dst in zip(rest[:n_casts], rest[n_casts + 1:2 * n_casts + 1]):
        dst[...] = src[...].astype(BF16)

    @pl.when((pl.program_id(0) == 0) & (pl.program_id(1) == 0))
    def _():
        for w_ref, w_s in ((wa_ref, wa_s), (wc_ref, wc_s), (wo_ref, wo_s)):
            w_s[...] = w_ref[...].astype(BF16)

    tile_start = pl.program_id(1) * q_tile
    lane = lax.broadcasted_iota(jnp.int32, (2 * BLOCK, LANES), 1)
    lo = lane < HEAD_DIM
    tiles_per_kv = ATTN_W // LANES // N_KV_HEADS
    qi = lax.broadcasted_iota(jnp.int32, (tiles_per_kv * BLOCK, 2 * BLOCK), 0) % BLOCK
    kj = lax.broadcasted_iota(jnp.int32, (tiles_per_kv * BLOCK, 2 * BLOCK), 1)
    delta = qi + BLOCK - kj
    band = (delta >= 0) & (delta < WINDOW)
    out_lo = lax.broadcasted_iota(jnp.int32, (tiles_per_kv * BLOCK, LANES), 1) < HEAD_DIM
    key_lo = lax.broadcasted_iota(jnp.int32, (2 * 2 * BLOCK, LANES), 0) < 2 * BLOCK
    ones_hat = (key_lo == out_lo).astype(BF16)

    c_parts, pending_zero = [], None
    for blk in range(q_tile // BLOCK):
        start = pl.multiple_of(tile_start + blk * BLOCK, BLOCK)
        prev = pl.multiple_of(jnp.maximum(start - BLOCK, 0), BLOCK)
        valid = band & (start + kj - BLOCK >= 0) if blk == 0 else band
        rows = pl.ds(blk * BLOCK, BLOCK)

        def band_of(ref):
            return jnp.concatenate([ref[pl.ds(prev, BLOCK), :], ref[pl.ds(start, BLOCK), :]], axis=0)

        kf, vf = band_of(k_ref), band_of(v_ref)
        kr, vr = pltpu.roll(kf, HEAD_DIM, axis=1), pltpu.roll(vf, HEAD_DIM, axis=1)
        zero = jnp.zeros_like(kf)
        for h in range(N_KV_HEADS):
            k_a, k_b = (kf, kr) if h == 0 else (kr, kf)
            v_a, v_b = (vf, vr) if h == 0 else (vr, vf)
            k_top = jnp.where(lo, k_a, zero)
            if pending_zero is not None:
                z = jnp.concatenate([pending_zero, pending_zero], axis=0).astype(BF16)
                k_top = jnp.concatenate([k_top[:2 * SUBLANES] + z, k_top[2 * SUBLANES:]], axis=0)
                pending_zero = None
            k_hat = jnp.concatenate([k_top, jnp.where(lo, zero, k_b)], axis=0)
            v_hat = jnp.concatenate([jnp.where(lo, v_a, zero), jnp.where(lo, zero, v_b)], axis=0)
            q_h = jnp.concatenate(
                [q_ref[rows, pl.ds((h * tiles_per_kv + c) * LANES, LANES)] for c in range(tiles_per_kv)], axis=0)
            s = lax.dot_general(q_h, k_hat, (((1,), (1,)), ((), ())), preferred_element_type=F32)
            probs, sink_terms = [], []
            for half in range(2):
                sh = jnp.where(valid, s[:, half * 2 * BLOCK:(half + 1) * 2 * BLOCK], NEG_INF)
                sink = jnp.concatenate(
                    [jnp.full((BLOCK, LANES), sinks_ref[2 * (h * tiles_per_kv + c) + half] * LOG2E, F32)
                     for c in range(tiles_per_kv)], axis=0)
                m = jnp.max(jnp.maximum(jnp.maximum(sh[:, :LANES], sh[:, LANES:]), sink), axis=-1, keepdims=True)
                probs.append(jnp.exp2(sh - m).astype(BF16))
                sink_terms.append(jnp.exp2(sink - m))
            o_aug = _dot(jnp.concatenate(probs, axis=1), jnp.concatenate([v_hat, ones_hat], axis=1))
            denom = o_aug[:, LANES:] + jnp.where(out_lo, sink_terms[0], sink_terms[1])
            o2 = (o_aug[:, :LANES] / denom).astype(BF16)
            for c in range(tiles_per_kv):
                attn_ref[rows, pl.ds((h * tiles_per_kv + c) * LANES, LANES)] = o2[c * BLOCK:(c + 1) * BLOCK]
            if len(c_parts) * MERGE_CHUNK < D_MODEL:
                col = len(c_parts) * MERGE_CHUNK
                c_parts.append(_dot(c_ref[...], wc_s[:, col:col + MERGE_CHUNK]))
                pending_zero = _zero_after([c_parts[-1][-SUBLANES:, -LANES:]])

    while len(c_parts) * MERGE_CHUNK < D_MODEL:
        col = len(c_parts) * MERGE_CHUNK
        c_parts.append(_dot(c_ref[...], wc_s[:, col:col + MERGE_CHUNK]))
    c = jnp.concatenate(c_parts, axis=1)
    a = _dot(attn_ref[...], wa_s[...])
    merged = (ga_ref[...].astype(F32) * a + gb_ref[...].astype(F32) * c).astype(BF16)
    o_ref[...] = x_ref[...] + _dot(merged, wo_s[...])


def _cast_rows(rows, steps):
    for n in range(steps, 0, -1):
        if rows % n == 0 and (rows // n) % (2 * SUBLANES) == 0:
            return rows // n
    raise ValueError(f"no row split of {rows} over {steps} steps")


def _attention_merge(x, q, k, v, sinks, c_act, ga, gb, wa, wc, wo, weights, batch, seq, q_tile):
    t = q.shape[0]
    tiles = seq // q_tile
    steps = batch * tiles
    cast_specs = []
    for w in weights:
        rp = _cast_rows(w.shape[0], steps)
        last = w.shape[0] // rp - 1
        cast_specs.append(pl.BlockSpec((rp, w.shape[1]),
                                       lambda b, i, s, last=last: (jnp.minimum(b * tiles + i, last), 0)))
    row = pl.BlockSpec((q_tile, D_MODEL), lambda b, i, s: (b * tiles + i, 0))
    kv = pl.BlockSpec((seq, KV_W), lambda b, i, s: (b, 0))
    mat = _const_spec((D_MODEL, D_MODEL))
    return pl.pallas_call(
        functools.partial(_attn_merge_kernel, q_tile=q_tile, n_casts=len(weights)),
        grid_spec=pltpu.PrefetchScalarGridSpec(
            num_scalar_prefetch=1,
            grid=(batch, tiles),
            in_specs=[row, kv, kv, row, row, row, row, mat, mat, mat] + cast_specs,
            out_specs=[row] + cast_specs,
            scratch_shapes=[pltpu.VMEM((q_tile, ATTN_W), BF16)] + [pltpu.VMEM((D_MODEL, D_MODEL), BF16)] * 3,
        ),
        out_shape=[jax.ShapeDtypeStruct((t, D_MODEL), F32)] + [jax.ShapeDtypeStruct(w.shape, BF16) for w in weights],
        compiler_params=_params(2),
        name="swa_merge",
    )(sinks, q, k, v, x, c_act, ga, gb, wa, wc, wo, *weights)


def _rope_tables(seq):
    half = ROPE_DIM // 2
    inv_freq = ROPE_THETA ** (-jnp.arange(0, ROPE_DIM, 2, dtype=F32) / ROPE_DIM)
    ang = jnp.arange(seq, dtype=F32)[:, None] * inv_freq[None, :]
    cos, sin = jnp.cos(ang), jnp.sin(ang)
    d = np.arange(LANES) % HEAD_DIM
    cos_l, sin_l = jnp.ones((seq, LANES), F32), jnp.zeros((seq, LANES), F32)
    for j in range(half):
        pick = (d < ROPE_DIM) & (d % half == j)
        cos_l = jnp.where(pick, cos[:, j:j + 1], cos_l)
        sin_l = jnp.where(pick, sin[:, j:j + 1], sin_l)
    return cos_l, jnp.where(d < half, -sin_l, 0.0), jnp.where(d >= half, sin_l, 0.0)


def _block_ones(width):
    head = np.arange(width) // HEAD_DIM
    return jnp.asarray(head[:, None] == head[None, :], BF16)


def kernel(x, p, ffn1_norm, ffn1_wgu, ffn1_wdown, mix_norm, w_in, q_norm, k_norm, sinks, attn_proj, dw_w, dw_b, conv_ln_g, conv_ln_b, conv_proj, w_out, ffn2_norm, ffn2_wgu, ffn2_wdown, ple_proj, ple_norm, ple_gate_norm, ple_gate_w):
    batch, seq, _ = x.shape
    depth = ffn1_norm.shape[0]
    t = batch * seq
    xt = x.reshape(t, D_MODEL)
    c_tab, s_fwd, s_bwd = _rope_tables(seq)
    ones_q, ones_k = _block_ones(2 * LANES), _block_ones(KV_W)
    vec = lambda a: a.reshape(1, -1)
    bf = lambda a: a.astype(BF16)
    for i in range(depth):
        xt, q, k, v, ga, gb, c_act = _front(
            xt, vec(ffn1_norm[i]), bf(ffn1_wgu[i]), bf(ffn1_wdown[i]), vec(mix_norm[i]), bf(w_in[i]),
            ones_q, ones_k, vec(jnp.tile(q_norm[i], N_Q_HEADS)), vec(jnp.tile(k_norm[i], N_KV_HEADS)),
            c_tab, s_fwd, s_bwd, dw_w[i], dw_b[i], vec(conv_ln_g[i]), vec(conv_ln_b[i]), FRONT_TILE, seq)
        later = [ffn2_wgu[i], ffn2_wdown[i], ple_proj[i], ple_gate_w[i]]
        xt, wgu2, wd2, wple, wgate = _attention_merge(xt, q, k, v, sinks[i], c_act, ga, gb, attn_proj[i],
                                                      conv_proj[i], w_out[i], later, batch, seq, TILE)
        xt = _ffn_ple(xt, p[i].reshape(t, PLE_DIM), vec(ffn2_norm[i]), wgu2, wd2,
                      wple, vec(ple_norm[i]), vec(ple_gate_norm[i]), wgate, TILE)
    return xt.reshape(batch, seq, D_MODEL)
```

```python
import functools

import jax
import jax.numpy as jnp
import numpy as np
from jax import lax
from jax.experimental import pallas as pl
from jax.experimental.pallas import tpu as pltpu

D_MODEL = 1024
N_Q_HEADS = 16
N_KV_HEADS = 2
HEAD_DIM = 64
WINDOW = 128
BLOCK = 128
ROPE_THETA = 500000.0
ROPE_DIM = HEAD_DIM // 4
ATTN_W = N_Q_HEADS * HEAD_DIM
KV_W = N_KV_HEADS * HEAD_DIM
CONV_WIDTH = 31
D_FF = 2816
PLE_DIM = 256
EPS = 1e-6
NEG_INF = -1e30
IN_K = ATTN_W
IN_C = IN_K + 2 * KV_W
IN_G = IN_C + 2 * D_MODEL
LOG2E = 1.4426950408889634

LANES = 128
SUBLANES = 8
assert D_MODEL == SUBLANES * LANES
CONV_HALO = 32
CONV_STEPS = 4
LN_ROWS = 16
MERGE_CHUNK = 256
FF_CHUNK = 256
assert D_FF % FF_CHUNK == 0
VMEM_LIMIT = 60 * 1024 * 1024
FRONT_TILE = 512
TILE = 512

BF16 = jnp.bfloat16
F32 = jnp.float32


def _dot(a, b):
    return jnp.dot(a, b, preferred_element_type=F32)


def _rmsnorm(x, g):
    return x * lax.rsqrt(jnp.mean(x * x, axis=-1, keepdims=True) + EPS) * g


def _const_spec(shape):
    return pl.BlockSpec(shape, lambda *_: (0,) * len(shape), pipeline_mode=pl.Buffered(1))


def _params(n_axes):
    return pltpu.CompilerParams(dimension_semantics=("arbitrary",) * n_axes, vmem_limit_bytes=VMEM_LIMIT)


def _swiglu_update(x, g_ref, wgu_ref, wd_ref):
    h = _rmsnorm(x, g_ref[...]).astype(BF16)
    g = _dot(h, wgu_ref[:, :D_FF])
    u = _dot(h, wgu_ref[:, D_FF:])
    a = (g * jax.nn.sigmoid(g) * u).astype(BF16)
    return x + 0.5 * _dot(a, wd_ref[...])


def _zero_after(values):
    tiles = [v[r:r + SUBLANES, c:c + LANES] for v in values
             for r in range(0, v.shape[0], SUBLANES) for c in range(0, v.shape[1], LANES)]
    half_word = jnp.uint32(16)
    bits = functools.reduce(jnp.bitwise_or, [pltpu.bitcast(t, jnp.uint32) for t in tiles])
    return lax.shift_right_logical(lax.shift_right_logical(bits, half_word), half_word).astype(F32)


class _Weaver:
    def __init__(self, lhs_ref, piece):
        self.lhs_ref, self.piece, self.zero = lhs_ref, piece, None

    def after_matmul(self, n=1):
        tile = (pl.ds(0, 2 * SUBLANES), pl.ds(0, LANES))
        if self.zero is not None:
            z = jnp.concatenate([self.zero, self.zero], axis=0)
            self.lhs_ref[tile] = (self.lhs_ref[tile].astype(F32) + z).astype(BF16)
        results = []
        for _ in range(n):
            results.extend(self.piece() or [])
        self.zero = _zero_after(results) if results else None


def _swiglu_update_chunked(x, g_ref, wgu_ref, wd_ref, weaver, n_pieces):
    h_ref = weaver.lhs_ref
    h_ref[...] = _rmsnorm(x, g_ref[...]).astype(BF16)
    y = None
    for c in range(0, D_FF, FF_CHUNK):
        g = _dot(h_ref[...], wgu_ref[:, c:c + FF_CHUNK])
        weaver.after_matmul(n_pieces)
        u = _dot(h_ref[...], wgu_ref[:, D_FF + c:D_FF + c + FF_CHUNK])
        weaver.after_matmul(n_pieces)
        a = (g * jax.nn.sigmoid(g) * u).astype(BF16)
        d = _dot(a, wd_ref[c:c + FF_CHUNK, :])
        weaver.after_matmul(n_pieces)
        y = d if y is None else y + d
    return x + 0.5 * y


def _ffn_ple_kernel(x_ref, p_ref, g_ref, wgu_ref, wd_ref, pw_ref, pn_ref, gn_ref, gw_ref, o_ref):
    x = _swiglu_update(x_ref[...], g_ref, wgu_ref, wd_ref)
    e = _rmsnorm(_dot(p_ref[...].astype(BF16), pw_ref[...]), pn_ref[...])
    gate = jax.nn.sigmoid(_dot(_rmsnorm(x, gn_ref[...]).astype(BF16), gw_ref[...]))
    o_ref[...] = x + gate * e


def _ffn_ple(x, p, norm, wgu, wd, ple_w, ple_norm, gate_norm, gate_w, tm):
    t = x.shape[0]
    row = pl.BlockSpec((tm, D_MODEL), lambda i: (i, 0))
    vec = _const_spec((1, D_MODEL))
    return pl.pallas_call(
        _ffn_ple_kernel,
        grid=(t // tm,),
        in_specs=[row, pl.BlockSpec((tm, PLE_DIM), lambda i: (i, 0)), vec,
                  _const_spec((D_MODEL, 2 * D_FF)), _const_spec((D_FF, D_MODEL)),
                  _const_spec((PLE_DIM, D_MODEL)), vec, vec, _const_spec((D_MODEL, D_MODEL))],
        out_specs=row,
        out_shape=jax.ShapeDtypeStruct((t, D_MODEL), F32),
        compiler_params=_params(1),
        name="ffn2_ple",
    )(x, p, norm, wgu, wd, ple_w, ple_norm, gate_norm, gate_w)


def _head_norm_rope(z, ones_blk, gain, c_tab, s_fwd, s_bwd):
    w = z.shape[1]
    blk = ones_blk.shape[0]
    sq = (z * z).astype(BF16)
    ss = jnp.concatenate([_dot(sq[:, i:i + blk], ones_blk) for i in range(0, w, blk)], axis=1)
    t = z * lax.rsqrt(ss * (1.0 / HEAD_DIM) + EPS) * gain
    half = ROPE_DIM // 2
    up = pltpu.roll(t, w - half, axis=1)
    dn = pltpu.roll(t, half, axis=1)
    reps = w // LANES
    tile = lambda a: jnp.concatenate([a] * reps, axis=1) if reps > 1 else a
    return t * tile(c_tab) + up * tile(s_fwd) + dn * tile(s_bwd)


def _conv_ln_silu(tmaj_ref, y_ref, dw_ref, db_ref, lg_ref, lb_ref, out_ref, tm):
    base = CONV_HALO - (CONV_WIDTH - 1)

    def conv_piece(t0):
        acc = [None] * CONV_STEPS
        for j in range(CONV_WIDTH):
            w_j = dw_ref[pl.ds(j * SUBLANES, SUBLANES), :]
            for r in range(CONV_STEPS):
                v = tmaj_ref[pl.ds((t0 + r + base + j) * SUBLANES, SUBLANES), :] * w_j
                acc[r] = v if j == 0 else acc[r] + v
        bias = db_ref[...]
        acc = [a + bias for a in acc]
        for r in range(CONV_STEPS):
            y_ref[pl.ds((t0 + r) * SUBLANES, SUBLANES), :] = acc[r]
        return acc

    def ln_piece(t0):
        y = jnp.concatenate([y_ref[pl.ds(t0 * SUBLANES + c, LN_ROWS, stride=SUBLANES), :]
                             for c in range(SUBLANES)], axis=1)
        yc = y - jnp.mean(y, axis=-1, keepdims=True)
        var = jnp.mean(yc * yc, axis=-1, keepdims=True)
        ln = yc * lax.rsqrt(var + EPS) * lg_ref[...] + lb_ref[...]
        act = ln * jax.nn.sigmoid(ln)
        out_ref[pl.ds(t0, LN_ROWS), :] = act.astype(BF16)
        return [act]

    pending = [functools.partial(conv_piece, t0) for t0 in range(0, tm, CONV_STEPS)]
    pending += [functools.partial(ln_piece, t0) for t0 in range(0, tm, LN_ROWS)]

    def piece():
        return pending.pop(0)() if pending else None

    def finish():
        while pending:
            piece()

    return piece, finish


def _front_kernel(x_ref, n1_ref, wgu_ref, wd_ref, nm_ref, win_ref,
                  oq_ref, ok_ref, gq_ref, gk_ref, c_ref, sf_ref, sb_ref, dw_ref, db_ref, lg_ref, lb_ref,
                  x1_out, q_out, k_out, v_out, ga_out, gb_out, cact_out, tmaj_ref, y_ref, h_ref, *, tm, tiles_per_seq):
    step = pl.program_id(0)
    halo_rows = CONV_HALO * SUBLANES

    @pl.when(step == 0)
    def _():
        tmaj_ref[...] = jnp.zeros(tmaj_ref.shape, F32)

    conv = functools.partial(_conv_ln_silu, tmaj_ref, y_ref, dw_ref, db_ref, lg_ref, lb_ref, cact_out, tm)
    last = pl.num_programs(0) - 1

    @pl.when(step == last)
    def _():
        conv()[1]()

    @pl.when(step < last)
    def _():
        conv_piece, conv_finish = conv()
        weaver = _Weaver(h_ref, conv_piece)
        ffn_matmuls = 3 * (D_FF // FF_CHUNK)
        x1 = _swiglu_update_chunked(x_ref[...], n1_ref, wgu_ref, wd_ref, weaver,
                                    pl.cdiv(tm // CONV_STEPS, ffn_matmuls))
        k = tm // (8 * LN_ROWS)
        x1_out[...] = x1
        h_ref[...] = _rmsnorm(x1, nm_ref[...]).astype(BF16)
        c_tab, s_fwd, s_bwd = c_ref[...], sf_ref[...], sb_ref[...]
        zq = _dot(h_ref[...], win_ref[:, :IN_K])
        weaver.after_matmul(k)
        scale = HEAD_DIM ** -0.5 * LOG2E
        q_out[...] = (_head_norm_rope(zq, oq_ref[...], gq_ref[...], c_tab, s_fwd, s_bwd) * scale).astype(BF16)
        zkv = _dot(h_ref[...], win_ref[:, IN_K:IN_C])
        weaver.after_matmul(k)
        k_out[...] = _head_norm_rope(zkv[:, :KV_W], ok_ref[...], gk_ref[...], c_tab, s_fwd, s_bwd).astype(BF16)
        v_out[...] = zkv[:, KV_W:].astype(BF16)
        za = _dot(h_ref[...], win_ref[:, IN_G:IN_G + D_MODEL])
        weaver.after_matmul(2 * k)
        ga_out[...] = jax.nn.sigmoid(za).astype(BF16)
        zb = _dot(h_ref[...], win_ref[:, IN_G + D_MODEL:])
        weaver.after_matmul(2 * k)
        gb_out[...] = jax.nn.sigmoid(zb).astype(BF16)
        u_lin = _dot(h_ref[...], win_ref[:, IN_C:IN_C + D_MODEL])
        weaver.after_matmul(2 * k)
        conv_finish()
        u = u_lin * jax.nn.sigmoid(_dot(h_ref[...], win_ref[:, IN_C + D_MODEL:IN_G]))

        tail = tmaj_ref[pl.ds(tm * SUBLANES, halo_rows), :]
        tmaj_ref[pl.ds(0, halo_rows), :] = jnp.where(step % tiles_per_seq == 0, jnp.zeros_like(tail), tail)
        for c in range(SUBLANES):
            tmaj_ref[pl.ds(halo_rows + c, tm, stride=SUBLANES), :] = u[:, c * LANES:(c + 1) * LANES]


def _front(x, n1, wgu, wd, nm, w_in, ones_q, ones_k, gq, gk, c_tab, s_fwd, s_bwd,
           dw_w, dw_b, ln_g, ln_b, tm, seq):
    t = x.shape[0]
    n_tiles = t // tm
    tiles_per_seq = seq // tm
    cur = lambda s: jnp.minimum(s, n_tiles - 1)
    row = lambda w: pl.BlockSpec((tm, w), lambda s: (cur(s), 0))
    lag = pl.BlockSpec((tm, D_MODEL), lambda s: (jnp.maximum(s - 1, 0), 0))
    tab = pl.BlockSpec((tm, LANES), lambda s: (cur(s) % tiles_per_seq, 0))
    vec = _const_spec((1, D_MODEL))
    out = lambda w, dt=BF16: jax.ShapeDtypeStruct((t, w), dt)
    dw_tmaj = dw_w.reshape(CONV_WIDTH * SUBLANES, LANES)
    db_tmaj = dw_b.reshape(SUBLANES, LANES)
    return pl.pallas_call(
        functools.partial(_front_kernel, tm=tm, tiles_per_seq=tiles_per_seq),
        grid=(n_tiles + 1,),
        in_specs=[row(D_MODEL), vec, _const_spec(wgu.shape), _const_spec(wd.shape), vec, _const_spec(w_in.shape),
                  _const_spec(ones_q.shape), _const_spec(ones_k.shape),
                  _const_spec((1, ATTN_W)), _const_spec((1, KV_W)), tab, tab, tab,
                  _const_spec(dw_tmaj.shape), _const_spec(db_tmaj.shape), vec, vec],
        out_specs=[row(D_MODEL), row(ATTN_W), row(KV_W), row(KV_W), row(D_MODEL), row(D_MODEL), lag],
        out_shape=[out(D_MODEL, F32), out(ATTN_W), out(KV_W), out(KV_W), out(D_MODEL), out(D_MODEL),
                   out(D_MODEL)],
        scratch_shapes=[pltpu.VMEM(((CONV_HALO + tm) * SUBLANES, LANES), F32),
                        pltpu.VMEM((tm * SUBLANES, LANES), F32),
                        pltpu.VMEM((tm, D_MODEL), BF16)],
        compiler_params=_params(1),
        name="ffn1_proj_conv",
    )(x, n1, wgu, wd, nm, w_in, ones_q, ones_k, gq, gk, c_tab, s_fwd, s_bwd,
      dw_tmaj, db_tmaj, ln_g, ln_b)


def _attn_merge_kernel(sinks_ref, q_ref, k_ref, v_ref, x_ref, c_ref, ga_ref, gb_ref, wa_ref, wc_ref, wo_ref,
                       *rest, q_tile, n_casts):
    o_ref = rest[n_casts]
    attn_ref, wa_s, wc_s, wo_s = rest[2 * n_casts + 1:]
    for src, dst in zip(rest[:n_casts], rest[n_casts + 1:2 * n_casts + 1]):
        dst[...] = src[...].astype(BF16)

    @pl.when((pl.program_id(0) == 0) & (pl.program_id(1) == 0))
    def _():
        for w_ref, w_s in ((wa_ref, wa_s), (wc_ref, wc_s), (wo_ref, wo_s)):
            w_s[...] = w_ref[...].astype(BF16)

    tile_start = pl.program_id(1) * q_tile
    lane = lax.broadcasted_iota(jnp.int32, (2 * BLOCK, LANES), 1)
    lo = lane < HEAD_DIM
    tiles_per_kv = ATTN_W // LANES // N_KV_HEADS
    qi = lax.broadcasted_iota(jnp.int32, (tiles_per_kv * BLOCK, 2 * BLOCK), 0) % BLOCK
    kj = lax.broadcasted_iota(jnp.int32, (tiles_per_kv * BLOCK, 2 * BLOCK), 1)
    delta = qi + BLOCK - kj
    band = (delta >= 0) & (delta < WINDOW)
    out_lo = lax.broadcasted_iota(jnp.int32, (tiles_per_kv * BLOCK, LANES), 1) < HEAD_DIM
    key_lo = lax.broadcasted_iota(jnp.int32, (2 * 2 * BLOCK, LANES), 0) < 2 * BLOCK
    ones_hat = (key_lo == out_lo).astype(BF16)

    c_parts, pending_zero = [], None
    for blk in range(q_tile // BLOCK):
        start = pl.multiple_of(tile_start + blk * BLOCK, BLOCK)
        prev = pl.multiple_of(jnp.maximum(start - BLOCK, 0), BLOCK)
        valid = band & (start + kj - BLOCK >= 0) if blk == 0 else band
        rows = pl.ds(blk * BLOCK, BLOCK)

        def band_of(ref):
            return jnp.concatenate([ref[pl.ds(prev, BLOCK), :], ref[pl.ds(start, BLOCK), :]], axis=0)

        kf, vf = band_of(k_ref), band_of(v_ref)
        kr, vr = pltpu.roll(kf, HEAD_DIM, axis=1), pltpu.roll(vf, HEAD_DIM, axis=1)
        zero = jnp.zeros_like(kf)
        for h in range(N_KV_HEADS):
            k_a, k_b = (kf, kr) if h == 0 else (kr, kf)
            v_a, v_b = (vf, vr) if h == 0 else (vr, vf)
            k_top = jnp.where(lo, k_a, zero)
            if pending_zero is not None:
                z = jnp.concatenate([pending_zero, pending_zero], axis=0).astype(BF16)
                k_top = jnp.concatenate([k_top[:2 * SUBLANES] + z, k_top[2 * SUBLANES:]], axis=0)
                pending_zero = None
            k_hat = jnp.concatenate([k_top, jnp.where(lo, zero, k_b)], axis=0)
            v_hat = jnp.concatenate([jnp.where(lo, v_a, zero), jnp.where(lo, zero, v_b)], axis=0)
            q_h = jnp.concatenate(
                [q_ref[rows, pl.ds((h * tiles_per_kv + c) * LANES, LANES)] for c in range(tiles_per_kv)], axis=0)
            s = lax.dot_general(q_h, k_hat, (((1,), (1,)), ((), ())), preferred_element_type=F32)
            probs, sink_terms = [], []
            for half in range(2):
                sh = jnp.where(valid, s[:, half * 2 * BLOCK:(half + 1) * 2 * BLOCK], NEG_INF)
                sink = jnp.concatenate(
                    [jnp.full((BLOCK, LANES), sinks_ref[2 * (h * tiles_per_kv + c) + half] * LOG2E, F32)
                     for c in range(tiles_per_kv)], axis=0)
                m = jnp.max(jnp.maximum(jnp.maximum(sh[:, :LANES], sh[:, LANES:]), sink), axis=-1, keepdims=True)
                probs.append(jnp.exp2(sh - m).astype(BF16))
                sink_terms.append(jnp.exp2(sink - m))
            o_aug = _dot(jnp.concatenate(probs, axis=1), jnp.concatenate([v_hat, ones_hat], axis=1))
            denom = o_aug[:, LANES:] + jnp.where(out_lo, sink_terms[0], sink_terms[1])
            o2 = (o_aug[:, :LANES] / denom).astype(BF16)
            for c in range(tiles_per_kv):
                attn_ref[rows, pl.ds((h * tiles_per_kv + c) * LANES, LANES)] = o2[c * BLOCK:(c + 1) * BLOCK]
            if len(c_parts) * MERGE_CHUNK < D_MODEL:
                col = len(c_parts) * MERGE_CHUNK
                c_parts.append(_dot(c_ref[...], wc_s[:, col:col + MERGE_CHUNK]))
                pending_zero = _zero_after([c_parts[-1][-SUBLANES:, -LANES:]])

    while len(c_parts) * MERGE_CHUNK < D_MODEL:
        col = len(c_parts) * MERGE_CHUNK
        c_parts.append(_dot(c_ref[...], wc_s[:, col:col + MERGE_CHUNK]))
    c = jnp.concatenate(c_parts, axis=1)
    a = _dot(attn_ref[...], wa_s[...])
    merged = (ga_ref[...].astype(F32) * a + gb_ref[...].astype(F32) * c).astype(BF16)
    o_ref[...] = x_ref[...] + _dot(merged, wo_s[...])


def _cast_rows(rows, steps):
    for n in range(steps, 0, -1):
        if rows % n == 0 and (rows // n) % (2 * SUBLANES) == 0:
            return rows // n
    raise ValueError(f"no row split of {rows} over {steps} steps")


def _attention_merge(x, q, k, v, sinks, c_act, ga, gb, wa, wc, wo, weights, batch, seq, q_tile):
    t = q.shape[0]
    tiles = seq // q_tile
    steps = batch * tiles
    cast_specs = []
    for w in weights:
        rp = _cast_rows(w.shape[0], steps)
        last = w.shape[0] // rp - 1
        cast_specs.append(pl.BlockSpec((rp, w.shape[1]),
                                       lambda b, i, s, last=last: (jnp.minimum(b * tiles + i, last), 0)))
    row = pl.BlockSpec((q_tile, D_MODEL), lambda b, i, s: (b * tiles + i, 0))
    kv = pl.BlockSpec((seq, KV_W), lambda b, i, s: (b, 0))
    mat = _const_spec((D_MODEL, D_MODEL))
    return pl.pallas_call(
        functools.partial(_attn_merge_kernel, q_tile=q_tile, n_casts=len(weights)),
        grid_spec=pltpu.PrefetchScalarGridSpec(
            num_scalar_prefetch=1,
            grid=(batch, tiles),
            in_specs=[row, kv, kv, row, row, row, row, mat, mat, mat] + cast_specs,
            out_specs=[row] + cast_specs,
            scratch_shapes=[pltpu.VMEM((q_tile, ATTN_W), BF16)] + [pltpu.VMEM((D_MODEL, D_MODEL), BF16)] * 3,
        ),
        out_shape=[jax.ShapeDtypeStruct((t, D_MODEL), F32)] + [jax.ShapeDtypeStruct(w.shape, BF16) for w in weights],
        compiler_params=_params(2),
        name="swa_merge",
    )(sinks, q, k, v, x, c_act, ga, gb, wa, wc, wo, *weights)


def _rope_tables(seq):
    half = ROPE_DIM // 2
    inv_freq = ROPE_THETA ** (-jnp.arange(0, ROPE_DIM, 2, dtype=F32) / ROPE_DIM)
    ang = jnp.arange(seq, dtype=F32)[:, None] * inv_freq[None, :]
    cos, sin = jnp.cos(ang), jnp.sin(ang)
    d = np.arange(LANES) % HEAD_DIM
    cos_l, sin_l = jnp.ones((seq, LANES), F32), jnp.zeros((seq, LANES), F32)
    for j in range(half):
        pick = (d < ROPE_DIM) & (d % half == j)
        cos_l = jnp.where(pick, cos[:, j:j + 1], cos_l)
        sin_l = jnp.where(pick, sin[:, j:j + 1], sin_l)
    return cos_l, jnp.where(d < half, -sin_l, 0.0), jnp.where(d >= half, sin_l, 0.0)


def _block_ones(width):
    head = np.arange(width) // HEAD_DIM
    return jnp.asarray(head[:, None] == head[None, :], BF16)


def kernel(x, p, ffn1_norm, ffn1_wgu, ffn1_wdown, mix_norm, w_in, q_norm, k_norm, sinks, attn_proj, dw_w, dw_b, conv_ln_g, conv_ln_b, conv_proj, w_out, ffn2_norm, ffn2_wgu, ffn2_wdown, ple_proj, ple_norm, ple_gate_norm, ple_gate_w):
    batch, seq, _ = x.shape
    depth = ffn1_norm.shape[0]
    t = batch * seq
    xt = x.reshape(t, D_MODEL)
    c_tab, s_fwd, s_bwd = _rope_tables(seq)
    ones_q, ones_k = _block_ones(2 * LANES), _block_ones(KV_W)
    vec = lambda a: a.reshape(1, -1)
    bf = lambda a: a.astype(BF16)
    for i in range(depth):
        xt, q, k, v, ga, gb, c_act = _front(
            xt, vec(ffn1_norm[i]), bf(ffn1_wgu[i]), bf(ffn1_wdown[i]), vec(mix_norm[i]), bf(w_in[i]),
            ones_q, ones_k, vec(jnp.tile(q_norm[i], N_Q_HEADS)), vec(jnp.tile(k_norm[i], N_KV_HEADS)),
            c_tab, s_fwd, s_bwd, dw_w[i], dw_b[i], vec(conv_ln_g[i]), vec(conv_ln_b[i]), FRONT_TILE, seq)
        later = [ffn2_wgu[i], ffn2_wdown[i], ple_proj[i], ple_gate_w[i]]
        xt, wgu2, wd2, wple, wgate = _attention_merge(xt, q, k, v, sinks[i], c_act, ga, gb, attn_proj[i],
                                                      conv_proj[i], w_out[i], later, batch, seq, TILE)
        xt = _ffn_ple(xt, p[i].reshape(t, PLE_DIM), vec(ffn2_norm[i]), wgu2, wd2,
                      wple, vec(ple_norm[i]), vec(ple_gate_norm[i]), wgate, TILE)
    return xt.reshape(batch, seq, D_MODEL)
```
